```python
import math, functools
import jax, jax.numpy as jnp
from jax import lax
import numpy as np

D_MODEL = 1024
BATCH = 16
SEQ = 4096
DEPTH = 2

GRID_W = 64
CTX_LEN = 256
EPS = 1e-6
N_MOD = 6

SSD_HEADDIM = 64
D_SSD = D_MODEL
SSD_HEADS = D_SSD // SSD_HEADDIM
SSD_GROUPS = 2
SSD_HPG = SSD_HEADS // SSD_GROUPS
SSD_STATE = 128
SSD_GN = SSD_GROUPS * SSD_STATE
D_XBC = D_SSD + 2 * SSD_GN
SSD_CHUNK = 128

CONV_K = 4
CONV_PAD_L = 2

D_LRU = D_MODEL
LRU_BW = 64
LRU_BLOCKS = D_LRU // LRU_BW
RGLRU_C = 8.0

D_MIX = D_SSD + D_LRU
OFF_XBC = D_SSD
OFF_DT = OFF_XBC + D_XBC
OFF_LRU_X = OFF_DT + 2 * SSD_HEADS
OFF_LRU_G = OFF_LRU_X + D_LRU
D_IN = OFF_LRU_G + D_LRU

D_FF = ((8 * D_MODEL // 3 + 127) // 128) * 128
N_EXPERTS = 8
TOP_K = 2
N_DENSE = (DEPTH + 1) // 2
N_MOE = DEPTH // 2

kernel_name = "hybrid_ssd_rglru_moe_dit_prefix"


def rmsnorm(x, w):
    xf = x.astype(jnp.float32)
    y = xf * lax.rsqrt(jnp.mean(jnp.square(xf), axis=-1, keepdims=True) + EPS)
    return (y * w.astype(jnp.float32)).astype(x.dtype)


def modulate(h, shift, scale):
    return h * (1 + scale) + shift


def dwconv(x, w, b):
    ch = x.shape[-1]
    y = lax.conv_general_dilated(x, w[:, None, :].astype(x.dtype), window_strides=(1,),
                                 padding=[(CONV_PAD_L, CONV_K - 1 - CONV_PAD_L)],
                                 dimension_numbers=("NWC", "WIO", "NWC"), feature_group_count=ch)
    return y + b


def segsum(a):
    t = a.shape[-1]
    cs = jnp.cumsum(a, axis=-1)
    diff = cs[..., :, None] - cs[..., None, :]
    return jnp.where(jnp.tril(jnp.ones((t, t), dtype=bool)), diff, -jnp.inf)


def ssd_chunked(X, dA, Bm, Cm, init):
    b, t = X.shape[:2]
    nc = t // SSD_CHUNK
    Xc = X.reshape(b, nc, SSD_CHUNK, SSD_GROUPS, SSD_HPG, SSD_HEADDIM)
    Bc = Bm.reshape(b, nc, SSD_CHUNK, SSD_GROUPS, SSD_STATE)
    Cc = Cm.reshape(b, nc, SSD_CHUNK, SSD_GROUPS, SSD_STATE)
    Ac = dA.reshape(b, nc, SSD_CHUNK, SSD_GROUPS, SSD_HPG).transpose(0, 3, 4, 1, 2)
    A_cs = jnp.cumsum(Ac, axis=-1)
    L = jnp.exp(segsum(Ac))
    CB = jnp.einsum("bclgn,bcsgn->bgcls", Cc, Bc)
    Y_diag = jnp.einsum("bgrcls,bcsgrp->bclgrp", CB[:, :, None] * L, Xc)
    decay_states = jnp.exp(A_cs[..., -1:] - A_cs)
    Xd = Xc * decay_states.transpose(0, 3, 4, 1, 2)[..., None]
    states = jnp.einsum("bclgn,bclgrp->bcgrpn", Bc, Xd)
    states = jnp.concatenate([init[:, None], states], axis=1)
    chunk_tot = jnp.pad(A_cs[..., -1], ((0, 0), (0, 0), (0, 0), (1, 0)))
    decay_chunk = jnp.exp(segsum(chunk_tot))
    new_states = jnp.einsum("bgrzc,bcgrpn->bzgrpn", decay_chunk, states)
    states_in, final = new_states[:, :-1], new_states[:, -1]
    Y_off = jnp.einsum("bclgn,bcgrpn->bclgrp", Cc, states_in) * jnp.exp(A_cs).transpose(0, 3, 4, 1, 2)[..., None]
    return (Y_diag + Y_off).reshape(b, t, SSD_GROUPS, SSD_HPG, SSD_HEADDIM), final


def ssd_branch(z, xbc_raw, dt_raw, conv_w, conv_b, dt_bias, a_log, d_skip, norm_w, init_f, init_b):
    f32 = jnp.float32
    bsz, t, _ = z.shape
    xbc = jax.nn.silu(dwconv(xbc_raw, conv_w, conv_b))
    xf = xbc[..., :D_SSD].reshape(bsz, t, SSD_GROUPS, SSD_HPG, SSD_HEADDIM).astype(f32)
    Bm = xbc[..., D_SSD:D_SSD + SSD_GN].reshape(bsz, t, SSD_GROUPS, SSD_STATE).astype(f32)
    Cm = xbc[..., D_SSD + SSD_GN:].reshape(bsz, t, SSD_GROUPS, SSD_STATE).astype(f32)
    dt = jax.nn.softplus(dt_raw.astype(f32).reshape(bsz, t, 2, SSD_GROUPS, SSD_HPG)
                         + dt_bias.astype(f32).reshape(2, SSD_GROUPS, SSD_HPG))
    A = -jnp.exp(a_log.astype(f32)).reshape(2, SSD_GROUPS, SSD_HPG)
    flip = lambda v: jnp.flip(v, axis=1)
    y_f, s_f = ssd_chunked(xf * dt[:, :, 0, ..., None], dt[:, :, 0] * A[0], Bm, Cm, init_f)
    y_b, s_b = ssd_chunked(flip(xf * dt[:, :, 1, ..., None]), flip(dt[:, :, 1] * A[1]), flip(Bm), flip(Cm), init_b)
    y = y_f + flip(y_b) + d_skip.astype(f32).reshape(SSD_GROUPS, SSD_HPG, 1) * xf
    y = y.reshape(bsz, t, D_SSD).astype(z.dtype)
    return rmsnorm(y * jax.nn.silu(z), norm_w), s_f, s_b


def linear_combine(e1, e2):
    a1, b1 = e1
    a2, b2 = e2
    return a1 * a2, a2 * b1 + b2


def rglru_dir(xc, rw, rb, iw, ib, lam, h0):
    f32 = jnp.float32
    bsz, t, w = xc.shape
    xb = xc.reshape(bsz, t, LRU_BLOCKS, LRU_BW)
    r = jax.nn.sigmoid((jnp.einsum("btnk,nkj->btnj", xb, rw).reshape(bsz, t, w) + rb).astype(f32))
    i = jax.nn.sigmoid((jnp.einsum("btnk,nkj->btnj", xb, iw).reshape(bsz, t, w) + ib).astype(f32))
    log_a = -RGLRU_C * r * jax.nn.softplus(-lam.astype(f32))
    a = jnp.exp(log_a)
    b_in = jnp.sqrt(-jnp.expm1(2.0 * log_a)) * i * xc.astype(f32)
    b_in = b_in.at[:, 0].add(a[:, 0] * h0)
    _, h = lax.associative_scan(linear_combine, (a, b_in), axis=1)
    return h, h[:, -1]


def rglru_bidir(xc, rw, rb, iw, ib, lam, h0_f, h0_b):
    h_f, s_f = rglru_dir(xc, rw[0], rb[0], iw[0], ib[0], lam[0], h0_f)
    h_b, s_b = rglru_dir(jnp.flip(xc, axis=1), rw[1], rb[1], iw[1], ib[1], lam[1], h0_b)
    return h_f + jnp.flip(h_b, axis=1), s_f, s_b


def hybrid_mixer(u_lat, u_ctx, rows, ssd_conv_w, ssd_conv_b, ssd_dt_bias, ssd_a_log, ssd_d, ssd_norm_w,
                 lru_conv_w, lru_conv_b, lru_rw, lru_rb, lru_iw, lru_ib, lru_lambda, lru_norm_w):
    f32 = jnp.float32
    bsz, t, _ = u_lat.shape
    bc = u_ctx.shape[0]

    def parts(u):
        return (u[..., :OFF_XBC], u[..., OFF_XBC:OFF_DT], u[..., OFF_DT:OFF_LRU_X],
                u[..., OFF_LRU_X:OFF_LRU_G], u[..., OFF_LRU_G:])

    zc, xbcc, dtc, lxc, lgc = parts(u_ctx)
    zl, xbcl, dtl, lxl, lgl = parts(u_lat)
    ssd_p = (ssd_conv_w, ssd_conv_b, ssd_dt_bias, ssd_a_log, ssd_d, ssd_norm_w)
    lru_p = (lru_rw, lru_rb, lru_iw, lru_ib, lru_lambda)

    zero_s = jnp.zeros((bc, SSD_GROUPS, SSD_HPG, SSD_HEADDIM, SSD_STATE), f32)
    yc_ssd, s_f, s_b = ssd_branch(zc, xbcc, dtc, *ssd_p, zero_s, zero_s)
    yl_ssd, _, _ = ssd_branch(zl, xbcl, dtl, *ssd_p, s_f, s_b)

    zero_h = jnp.zeros((bc, D_LRU), f32)
    hc, h_f, h_b = rglru_bidir(dwconv(lxc, lru_conv_w, lru_conv_b), *lru_p, zero_h, zero_h)
    yc_lru = rmsnorm(hc.astype(u_ctx.dtype) * jax.nn.gelu(lgc), lru_norm_w)
    cols = lxl.reshape(bsz, rows, GRID_W, D_LRU).transpose(0, 2, 1, 3).reshape(bsz * GRID_W, rows, D_LRU)
    xcol = dwconv(cols, lru_conv_w, lru_conv_b).reshape(bsz, t, D_LRU)
    hl, _, _ = rglru_bidir(xcol, *lru_p, h_f, h_b)
    hl = hl.reshape(bsz, GRID_W, rows, D_LRU).transpose(0, 2, 1, 3).reshape(bsz, t, D_LRU)
    yl_lru = rmsnorm(hl.astype(u_lat.dtype) * jax.nn.gelu(lgl), lru_norm_w)

    return (jnp.concatenate([yl_ssd, yl_lru], axis=-1), jnp.concatenate([yc_ssd, yc_lru], axis=-1))


def swiglu(h, w1, w3, w2):
    return (jax.nn.silu(h @ w1) * (h @ w3)) @ w2


def moe_ffn(h, router_w, w1, w3, w2):
    shp = h.shape
    tok = h.reshape(-1, shp[-1])
    logits = (tok @ router_w).astype(jnp.float32)
    top_v, top_i = lax.top_k(logits, TOP_K)
    top_p = jax.nn.softmax(top_v, axis=-1)
    combine = jnp.sum(top_p[..., None] * jax.nn.one_hot(top_i, N_EXPERTS, dtype=jnp.float32), axis=-2)
    combine = combine.astype(tok.dtype)
    out = jnp.zeros_like(tok)
    for e in range(N_EXPERTS):
        out = out + combine[:, e:e + 1] * swiglu(tok, w1[e], w3[e], w2[e])
    return out.reshape(shp)


def setup_inputs(seed: int = 0) -> dict:
    key = jax.random.key(seed)
    k = jax.random.split(key, 32)
    f32 = jnp.float32
    nrm = lambda i, shape, scale: scale * jax.random.normal(k[i], shape, f32)
    dt0 = jnp.exp(jax.random.uniform(k[11], (DEPTH, 2, SSD_HEADS), f32, math.log(1e-3), math.log(1e-1)))
    a0 = jax.random.uniform(k[21], (DEPTH, 2, D_LRU), f32, 0.9, 0.999)
    s0 = a0 ** (1.0 / RGLRU_C)
    return {
        "x": nrm(0, (BATCH, SEQ, D_MODEL), 1.0),
        "c": nrm(1, (BATCH, D_MODEL), 1.0),
        "ctx": nrm(2, (BATCH, CTX_LEN, D_MODEL), 1.0),
        "c_ctx": nrm(3, (D_MODEL,), 1.0),
        "mod_w": nrm(4, (DEPTH, D_MODEL, N_MOD * D_MODEL), 0.3 * D_MODEL ** -0.5),
        "mod_b": nrm(5, (DEPTH, N_MOD * D_MODEL), 0.02),
        "norm1_w": 1.0 + nrm(6, (DEPTH, D_MODEL), 0.1),
        "norm2_w": 1.0 + nrm(7, (DEPTH, D_MODEL), 0.1),
        "w_in": nrm(8, (DEPTH, D_MODEL, D_IN), D_MODEL ** -0.5),
        "ssd_conv_w": nrm(9, (DEPTH, CONV_K, D_XBC), 0.5),
        "ssd_conv_b": nrm(10, (DEPTH, D_XBC), 0.02),
        "ssd_dt_bias": dt0 + jnp.log(-jnp.expm1(-dt0)),
        "ssd_a_log": jnp.log(jax.random.uniform(k[12], (DEPTH, 2, SSD_HEADS), f32, 1.0, 16.0)),
        "ssd_d": 1.0 + nrm(13, (DEPTH, SSD_HEADS), 0.1),
        "ssd_norm_w": 1.0 + nrm(14, (DEPTH, D_SSD), 0.1),
        "lru_conv_w": nrm(15, (DEPTH, CONV_K, D_LRU), 0.5),
        "lru_conv_b": nrm(16, (DEPTH, D_LRU), 0.02),
        "lru_rw": nrm(17, (DEPTH, 2, LRU_BLOCKS, LRU_BW, LRU_BW), LRU_BW ** -0.5),
        "lru_rb": nrm(18, (DEPTH, 2, D_LRU), 0.1),
        "lru_iw": nrm(19, (DEPTH, 2, LRU_BLOCKS, LRU_BW, LRU_BW), LRU_BW ** -0.5),
        "lru_ib": nrm(20, (DEPTH, 2, D_LRU), 0.1),
        "lru_lambda": jnp.log(s0) - jnp.log1p(-s0),
        "lru_norm_w": 1.0 + nrm(22, (DEPTH, D_LRU), 0.1),
        "w_out": nrm(23, (DEPTH, D_MIX, D_MODEL), D_MIX ** -0.5),
        "ffn_w1": nrm(24, (N_DENSE, D_MODEL, D_FF), D_MODEL ** -0.5),
        "ffn_w3": nrm(25, (N_DENSE, D_MODEL, D_FF), D_MODEL ** -0.5),
        "ffn_w2": nrm(26, (N_DENSE, D_FF, D_MODEL), D_FF ** -0.5),
        "router_w": nrm(27, (N_MOE, D_MODEL, N_EXPERTS), D_MODEL ** -0.5),
        "moe_w1": nrm(28, (N_MOE, N_EXPERTS, D_MODEL, D_FF), D_MODEL ** -0.5),
        "moe_w3": nrm(29, (N_MOE, N_EXPERTS, D_MODEL, D_FF), D_MODEL ** -0.5),
        "moe_w2": nrm(30, (N_MOE, N_EXPERTS, D_FF, D_MODEL), D_FF ** -0.5),
        "final_norm_w": 1.0 + nrm(31, (D_MODEL,), 0.1),
    }


def reference(x, c, ctx, c_ctx, mod_w, mod_b, norm1_w, norm2_w, w_in, ssd_conv_w, ssd_conv_b, ssd_dt_bias,
              ssd_a_log, ssd_d, ssd_norm_w, lru_conv_w, lru_conv_b, lru_rw, lru_rb, lru_iw, lru_ib, lru_lambda,
              lru_norm_w, w_out, ffn_w1, ffn_w3, ffn_w2, router_w, moe_w1, moe_w3, moe_w2, final_norm_w):
    rows = x.shape[1] // GRID_W
    xc = ctx
    sc = jax.nn.silu(c)
    scc = jax.nn.silu(c_ctx)
    for l in range(DEPTH):
        last = l == DEPTH - 1
        mod = jnp.split((sc @ mod_w[l] + mod_b[l])[:, None, :], N_MOD, axis=-1)
        modc = jnp.split(scc @ mod_w[l] + mod_b[l], N_MOD, axis=-1)

        hx = modulate(rmsnorm(x, norm1_w[l]), mod[0], mod[1])
        hc = modulate(rmsnorm(xc, norm1_w[l]), modc[0], modc[1])
        ox, oc = hybrid_mixer(hx @ w_in[l], hc @ w_in[l], rows, ssd_conv_w[l], ssd_conv_b[l], ssd_dt_bias[l],
                              ssd_a_log[l], ssd_d[l], ssd_norm_w[l], lru_conv_w[l], lru_conv_b[l], lru_rw[l],
                              lru_rb[l], lru_iw[l], lru_ib[l], lru_lambda[l], lru_norm_w[l])
        x = x + mod[2] * (ox @ w_out[l])

        if l % 2 == 0:
            ffn = functools.partial(swiglu, w1=ffn_w1[l // 2], w3=ffn_w3[l // 2], w2=ffn_w2[l // 2])
        else:
            ffn = functools.partial(moe_ffn, router_w=router_w[l // 2], w1=moe_w1[l // 2],
                                    w3=moe_w3[l // 2], w2=moe_w2[l // 2])
        x = x + mod[5] * ffn(modulate(rmsnorm(x, norm2_w[l]), mod[3], mod[4]))

        if not last:
            xc = xc + modc[2] * (oc @ w_out[l])
            xc = xc + modc[5] * ffn(modulate(rmsnorm(xc, norm2_w[l]), modc[3], modc[4]))
    return rmsnorm(x, final_norm_w)
```

```python
import functools

import jax
import jax.numpy as jnp
from jax import lax
from jax.experimental import pallas as pl
from jax.experimental.pallas import tpu as pltpu

F32 = jnp.float32
BF16 = jnp.bfloat16
HI = lax.Precision.HIGHEST

EPS = 1e-6
N_MOD = 6
GRID_W = 64
SSD_HEADDIM = 64
SSD_GROUPS = 2
SSD_STATE = 128
SSD_CHUNK = 128
CONV_K = 4
LRU_BW = 64
RGLRU_C = 8.0
N_EXPERTS = 8
LANE = 128
MOD_ROWS = 24
VMEM_LIMIT = 56 * 1024 * 1024


def _cparams(*sem):
    return pltpu.CompilerParams(dimension_semantics=sem, vmem_limit_bytes=VMEM_LIMIT)


def _resident(shape):
    nd = len(shape)
    return pl.BlockSpec(shape, lambda *_: (0,) * nd, pipeline_mode=pl.Buffered(1))


def _rms(x):
    return x * lax.rsqrt(jnp.mean(x * x, axis=-1, keepdims=True) + EPS)


def _mod_kernel(s_ref, w_ref, b_ref, o_ref):
    s = s_ref[...]
    s = s * jax.nn.sigmoid(s)
    o_ref[0] = jnp.dot(s, w_ref[0], precision=HI, preferred_element_type=F32) + b_ref[0]


def _modulation(c, c_ctx, mod_w, mod_b):
    depth, d, n = mod_w.shape
    bsz = c.shape[0]
    assert bsz + 1 <= MOD_ROWS
    s = jnp.concatenate([c, c_ctx[None], jnp.zeros((MOD_ROWS - bsz - 1, d), F32)], axis=0)
    tn = n // 4
    out = pl.pallas_call(
        _mod_kernel,
        grid=(depth, n // tn),
        in_specs=[
            pl.BlockSpec((MOD_ROWS, d), lambda l, j: (0, 0)),
            pl.BlockSpec((1, d, tn), lambda l, j: (l, 0, j)),
            pl.BlockSpec((1, 1, tn), lambda l, j: (l, 0, j)),
        ],
        out_specs=pl.BlockSpec((1, MOD_ROWS, tn), lambda l, j: (l, 0, j)),
        out_shape=jax.ShapeDtypeStruct((depth, MOD_ROWS, n), F32),
        compiler_params=_cparams("arbitrary", "arbitrary"),
        name="modulation",
    )(s, mod_w, mod_b[:, None, :])
    return out.reshape(depth, MOD_ROWS, N_MOD, d)


def _inproj_kernel(splits, dt_w, x_ref, mod_ref, nw_ref, w_ref, z_ref, xbc_ref, lx_ref, lg_ref, dt_ref):
    m = mod_ref[0]
    h = (_rms(x_ref[0]) * nw_ref[...] * (1.0 + m[1:2]) + m[0:1]).astype(BF16)

    def proj(a, b):
        return jnp.dot(h, w_ref[:, a:b], preferred_element_type=F32)

    o = [0]
    for s in splits:
        o.append(o[-1] + s)
    z_ref[0] = proj(o[0], o[1]).astype(BF16)
    xbc_ref[0] = proj(o[1], o[2]).astype(BF16)
    lx_ref[0] = proj(o[2], o[3]).astype(BF16)
    lg_ref[0] = proj(o[3], o[4]).astype(BF16)
    dt_ref[0] = proj(o[4], o[4] + LANE)[:, :dt_w]


def _inproj(x, mod_l, mod_row, norm_w, w_perm, splits, dt_w, tm):
    bsz, s, d = x.shape
    n_all = w_perm.shape[1]
    outs = [jax.ShapeDtypeStruct((bsz, s, w), BF16) for w in splits] + [jax.ShapeDtypeStruct((bsz, s, dt_w), F32)]
    return pl.pallas_call(
        functools.partial(_inproj_kernel, splits, dt_w),
        grid=(bsz, s // tm),
        in_specs=[
            pl.BlockSpec((1, tm, d), lambda b, i: (b, i, 0)),
            pl.BlockSpec((1, N_MOD, d), lambda b, i: (mod_row(b), 0, 0)),
            _resident((1, d)),
            _resident((d, n_all)),
        ],
        out_specs=[pl.BlockSpec((1, tm, w), lambda b, i: (b, i, 0)) for w in splits]
        + [pl.BlockSpec((1, tm, dt_w), lambda b, i: (b, i, 0))],
        out_shape=outs,
        compiler_params=_cparams("parallel", "arbitrary"),
        name="inproj",
    )(x, mod_l, norm_w[None], w_perm)


def _softplus(x):
    return jnp.maximum(x, 0.0) + jnp.log1p(jnp.exp(-jnp.abs(x)))


def _ssd_kernel(rev, nblk, tb, heads, *refs):
    if rev:
        (xbc_ref, xprev_ref, xnext_ref, dt_ref, cw_ref, cb_ref, dtb_ref, alog_ref, s0_ref,
         yf_ref, z_ref, dsk_ref, nw_ref, y_ref, sfin_ref, xp_ref, st_ref, yb_ref) = refs
    else:
        (xbc_ref, xprev_ref, xnext_ref, dt_ref, cw_ref, cb_ref, dtb_ref, alog_ref, s0_ref,
         y_ref, sfin_ref, xp_ref, st_ref) = refs
        yb_ref = y_ref.at[0]
    i = pl.program_id(1)
    blk = (nblk - 1 - i) if rev else i
    d_ssd = heads * SSD_HEADDIM
    gn = SSD_GROUPS * SSD_STATE
    gw = d_ssd // SSD_GROUPS
    ck = SSD_CHUNK

    @pl.when(i == 0)
    def _():
        st_ref[...] = s0_ref[0]

    xp_ref[8:8 + tb] = xbc_ref[0].astype(F32)
    xp_ref[0:8] = jnp.where(blk > 0, xprev_ref[0, 0].astype(F32), 0.0)
    xp_ref[8 + tb:16 + tb] = jnp.where(blk < nblk - 1, xnext_ref[0, 0].astype(F32), 0.0)
    acc = cb_ref[...] + cw_ref[0:1] * xp_ref[6:6 + tb]
    for k in range(1, CONV_K):
        acc = acc + cw_ref[k:k + 1] * xp_ref[6 + k:6 + k + tb]
    xs = acc * jax.nn.sigmoid(acc)

    d0 = heads if rev else 0
    dt = _softplus(dt_ref[0] + dtb_ref[...])[:, d0:d0 + heads]
    a_neg = -jnp.exp(alog_ref[...])[:, d0:d0 + heads]
    da = dt * a_neg
    ldt = jnp.where(dt > 0.0, jnp.log(dt), -1e30)

    ri = lax.broadcasted_iota(jnp.int32, (ck, ck), 0)
    ci = lax.broadcasted_iota(jnp.int32, (ck, ck), 1)
    tri = (ci >= ri) if rev else (ci <= ri)
    tri_f = tri.astype(F32)
    tri_t = ((ri >= ci) if rev else (ri <= ci)).astype(F32)
    eye = (ri == ci).astype(F32)
    rhs_rt = jnp.concatenate([tri_t, eye], axis=0)
    eh = lax.broadcasted_iota(jnp.int32, (heads, d_ssd), 0)
    ec = lax.broadcasted_iota(jnp.int32, (heads, d_ssd), 1)
    expand = jnp.where(ec // SSD_HEADDIM == eh, 1.0, 0.0).astype(BF16)
    expand2 = jnp.concatenate([expand, expand], axis=0)
    lane = lax.broadcasted_iota(jnp.int32, (ck, LANE), 1)
    end = 0 if rev else ck - 1

    def widen(q):
        hi = q.astype(BF16)
        lo = (q - hi.astype(F32)).astype(BF16)
        return jnp.dot(jnp.concatenate([hi, lo], axis=1), expand2, preferred_element_type=F32)

    order = range(tb // ck - 1, -1, -1) if rev else range(tb // ck)
    for c in order:
        r0 = c * ck
        xc = xs[r0:r0 + ck]
        x_bf = xc[:, :d_ssd].astype(BF16)
        b_bf = xc[:, d_ssd:d_ssd + gn].astype(BF16)
        c_bf = xc[:, d_ssd + gn:].astype(BF16)
        dac = da[r0:r0 + ck]
        cs = jnp.dot(tri_f, dac, precision=HI, preferred_element_type=F32)
        r_t = lax.dot_general(jnp.concatenate([dac, -ldt[r0:r0 + ck]], axis=0), rhs_rt,
                              (((0,), (0,)), ((), ())), precision=HI,
                              preferred_element_type=F32)
        ecs = jnp.exp(cs)
        tot = cs[end:end + 1]
        wgt = jnp.exp(tot - cs) * dt[r0:r0 + ck]
        ecs_w = widen(ecs)
        xw = (xc[:, :d_ssd] * widen(wgt)).astype(BF16)
        for g in range(SSD_GROUPS):
            bg = b_bf[:, g * SSD_STATE:(g + 1) * SSD_STATE]
            cg = c_bf[:, g * SSD_STATE:(g + 1) * SSD_STATE]
            cb = lax.dot_general(cg, bg, (((1,), (1,)), ((), ())), preferred_element_type=F32)
            s_in = st_ref[g]
            y_off = jnp.dot(cg, s_in.astype(BF16), preferred_element_type=F32) * ecs_w[:, g * gw:(g + 1) * gw]
            new = lax.dot_general(bg, xw[:, g * gw:(g + 1) * gw], (((0,), (0,)), ((), ())),
                                  preferred_element_type=F32)
            st_ref[g] = ecs_w[end:end + 1, g * gw:(g + 1) * gw] * s_in + new
            for p in range(gw // LANE):
                ha = (g * gw + p * LANE) // SSD_HEADDIM
                lm = []
                for h in (ha, ha + 1):
                    seg = cs[:, h:h + 1] - r_t[h:h + 1, :]
                    lm.append(jnp.where(tri, jnp.exp(seg), 0.0) * cb)
                lhs = jnp.concatenate(lm, axis=1).astype(BF16)
                xpair = x_bf[:, g * gw + p * LANE:g * gw + (p + 1) * LANE]
                zero = jnp.zeros_like(xpair)
                rhs = jnp.concatenate([jnp.where(lane < SSD_HEADDIM, xpair, zero),
                                       jnp.where(lane >= SSD_HEADDIM, xpair, zero)], axis=0)
                col = g * gw + p * LANE
                yb_ref[r0:r0 + ck, col:col + LANE] = (
                    jnp.dot(lhs, rhs, preferred_element_type=F32) + y_off[:, p * LANE:(p + 1) * LANE])

    if rev:
        z = z_ref[0].astype(F32)
        tot_y = (yb_ref[...] + yf_ref[0] + dsk_ref[...] * xs[:, :d_ssd]) * (z * jax.nn.sigmoid(z))
        y_ref[0] = (_rms(tot_y) * nw_ref[...]).astype(BF16)

    @pl.when(i == nblk - 1)
    def _():
        sfin_ref[0] = st_ref[...]


def _ssd_dir(rev, xbc, dt_raw, conv_w, conv_b, dt_bias, a_log, s0, tb, extra=None):
    bsz, s, dxbc = xbc.shape
    heads = dt_raw.shape[-1] // 2
    d_ssd = heads * SSD_HEADDIM
    gw = d_ssd // SSD_GROUPS
    nblk = s // tb
    nb8 = tb // 8
    xbc8 = xbc.reshape(bsz, s // 8, 8, dxbc)
    pos = (lambda i: nblk - 1 - i) if rev else (lambda i: i)
    in_specs = [
        pl.BlockSpec((1, tb, dxbc), lambda b, i: (b, pos(i), 0)),
        pl.BlockSpec((1, 1, 8, dxbc), lambda b, i: (b, jnp.maximum(pos(i) * nb8 - 1, 0), 0, 0)),
        pl.BlockSpec((1, 1, 8, dxbc), lambda b, i: (b, jnp.minimum((pos(i) + 1) * nb8, s // 8 - 1), 0, 0)),
        pl.BlockSpec((1, tb, 2 * heads), lambda b, i: (b, pos(i), 0)),
        _resident((CONV_K, dxbc)),
        _resident((1, dxbc)),
        _resident((1, 2 * heads)),
        _resident((1, 2 * heads)),
        pl.BlockSpec((1, SSD_GROUPS, SSD_STATE, gw), lambda b, i: (b, 0, 0, 0)),
    ]
    args = [xbc, xbc8, xbc8, dt_raw, conv_w, conv_b[None], dt_bias.reshape(1, -1), a_log.reshape(1, -1), s0]
    st_shape = jax.ShapeDtypeStruct((bsz, SSD_GROUPS, SSD_STATE, gw), F32)
    st_spec = pl.BlockSpec((1, SSD_GROUPS, SSD_STATE, gw), lambda b, i: (b, 0, 0, 0))
    scratch = [pltpu.VMEM((tb + 16, dxbc), F32), pltpu.VMEM((SSD_GROUPS, SSD_STATE, gw), F32)]
    if rev:
        yf, z, d_skip, norm_w = extra
        in_specs += [
            pl.BlockSpec((1, tb, d_ssd), lambda b, i: (b, pos(i), 0)),
            pl.BlockSpec((1, tb, d_ssd), lambda b, i: (b, pos(i), 0)),
            _resident((1, d_ssd)),
            _resident((1, d_ssd)),
        ]
        args += [yf, z, jnp.repeat(d_skip, SSD_HEADDIM)[None], norm_w[None]]
        y_dtype = BF16
        scratch.append(pltpu.VMEM((tb, d_ssd), F32))
    else:
        y_dtype = F32
    return pl.pallas_call(
        functools.partial(_ssd_kernel, rev, nblk, tb, heads),
        grid=(bsz, nblk),
        in_specs=in_specs,
        out_specs=[pl.BlockSpec((1, tb, d_ssd), lambda b, i: (b, pos(i), 0)), st_spec],
        out_shape=[jax.ShapeDtypeStruct((bsz, s, d_ssd), y_dtype), st_shape],
        scratch_shapes=scratch,
        compiler_params=_cparams("parallel", "arbitrary"),
        name="ssd_bwd" if rev else "ssd_fwd",
    )(*args)


def _ssd_bidir(z, xbc, dt_raw, p, s0_f, s0_b, tb):
    yf, s_f = _ssd_dir(False, xbc, dt_raw, p["conv_w"], p["conv_b"], p["dt_bias"], p["a_log"], s0_f, tb)
    y, s_b = _ssd_dir(True, xbc, dt_raw, p["conv_w"], p["conv_b"], p["dt_bias"], p["a_log"], s0_b, tb,
                      extra=(yf, z, p["d_skip"], p["norm_w"]))
    return y, s_f, s_b


def _shift_rows(x, s, fill, up):
    c = x.shape[0]
    rolled = pltpu.roll(x, (c - s) if up else s, 0)
    row = lax.broadcasted_iota(jnp.int32, x.shape, 0)
    keep = (row < c - s) if up else (row >= s)
    return jnp.where(keep, rolled, fill)


def _lru_kernel(nr, nc, rb, continuous, lx_ref, lg_ref, cw_ref, cb_ref, wg_ref, bg_ref, spl_ref, h0_ref,
                g_ref, hfin_ref, xp_ref, a_ref, b_ref):
    cwid = lx_ref.shape[-1]
    xin = lx_ref[0].astype(F32)
    xp_ref[2:nr + 2] = xin
    if continuous:
        for k in range(2):
            xp_ref[k] = _shift_rows(xin[nr - 2 + k], 1, 0.0, up=False)
        xp_ref[nr + 2] = _shift_rows(xin[0], 1, 0.0, up=True)
    else:
        xp_ref[0:2] = jnp.zeros((2, nc, cwid), F32)
        xp_ref[nr + 2:nr + 3] = jnp.zeros((1, nc, cwid), F32)

    def gates(t, _):
        r0 = pl.multiple_of(t * rb, rb)
        xc = cb_ref[...] + cw_ref[0:1] * xp_ref[pl.ds(r0, rb)]
        for k in range(1, CONV_K):
            xc = xc + cw_ref[k:k + 1] * xp_ref[pl.ds(r0 + k, rb)]
        xc2 = xc.reshape(rb * nc, cwid)
        gt = jnp.dot(xc2.astype(BF16), wg_ref[0], preferred_element_type=F32) + bg_ref[0]
        for d in range(2):
            r = jax.nn.sigmoid(gt[:, (2 * d) * cwid:(2 * d + 1) * cwid])
            ig = jax.nn.sigmoid(gt[:, (2 * d + 1) * cwid:(2 * d + 2) * cwid])
            log_a = -RGLRU_C * r * spl_ref[0, d:d + 1]
            a = jnp.exp(log_a)
            one_m_a2 = -jnp.tanh(log_a) * (a * a + 1.0)
            a_ref[d, pl.ds(r0, rb)] = a.reshape(rb, nc, cwid)
            b_ref[d, pl.ds(r0, rb)] = (jnp.sqrt(one_m_a2) * ig * xc2).reshape(rb, nc, cwid)
        return 0

    lax.fori_loop(0, nr // rb, gates, 0)

    hin = []
    for d in range(2):
        def step(t, carry):
            h, p = carry
            r = (nr - 1 - t) if d else t
            a = a_ref[d, r]
            h = a * h + b_ref[d, r]
            p = a * p
            b_ref[d, r] = h
            a_ref[d, r] = p
            return h, p

        h_l, p_l = lax.fori_loop(0, nr, step, (jnp.zeros((nc, cwid), F32), jnp.ones((nc, cwid), F32)))
        sa, sb = p_l, h_l
        s = 1
        while s < nc:
            sb = sa * _shift_rows(sb, s, 0.0, up=bool(d)) + sb
            sa = sa * _shift_rows(sa, s, 1.0, up=bool(d))
            s *= 2
        h0 = h0_ref[0, d:d + 1]
        after = sa * h0 + sb
        hin.append(_shift_rows(after, 1, h0, up=bool(d)) if nc > 1 else jnp.broadcast_to(h0, (nc, cwid)))
        last = 0 if d else nc - 1
        hfin_ref[0, d:d + 1] = after[last:last + 1]

    def finish(t, _):
        r0 = pl.multiple_of(t * rb, rb)
        h = (b_ref[0, pl.ds(r0, rb)] + a_ref[0, pl.ds(r0, rb)] * hin[0]
             + b_ref[1, pl.ds(r0, rb)] + a_ref[1, pl.ds(r0, rb)] * hin[1])
        g_ref[0, pl.ds(r0, rb)] = (h * jax.nn.gelu(lg_ref[0, pl.ds(r0, rb)].astype(F32))).astype(BF16)
        return 0

    lax.fori_loop(0, nr // rb, finish, 0)


def _lru(lx, lg, p, h0, rb, continuous):
    bsz, nr, nc, d = lx.shape
    cwid = p["wg"].shape[1]
    ng = d // cwid
    return pl.pallas_call(
        functools.partial(_lru_kernel, nr, nc, rb, continuous),
        grid=(bsz, ng),
        in_specs=[
            pl.BlockSpec((1, nr, nc, cwid), lambda b, j: (b, 0, 0, j)),
            pl.BlockSpec((1, nr, nc, cwid), lambda b, j: (b, 0, 0, j)),
            pl.BlockSpec((CONV_K, cwid), lambda b, j: (0, j)),
            pl.BlockSpec((1, cwid), lambda b, j: (0, j)),
            pl.BlockSpec((1, cwid, 4 * cwid), lambda b, j: (j, 0, 0)),
            pl.BlockSpec((1, 1, 4 * cwid), lambda b, j: (j, 0, 0)),
            pl.BlockSpec((1, 2, cwid), lambda b, j: (j, 0, 0)),
            pl.BlockSpec((1, 2, cwid), lambda b, j: (b, 0, j)),
        ],
        out_specs=[
            pl.BlockSpec((1, nr, nc, cwid), lambda b, j: (b, 0, 0, j)),
            pl.BlockSpec((1, 2, cwid), lambda b, j: (b, 0, j)),
        ],
        out_shape=[jax.ShapeDtypeStruct((bsz, nr, nc, d), BF16), jax.ShapeDtypeStruct((bsz, 2, d), F32)],
        scratch_shapes=[
            pltpu.VMEM((nr + 3, nc, cwid), F32),
            pltpu.VMEM((2, nr, nc, cwid), F32),
            pltpu.VMEM((2, nr, nc, cwid), F32),
        ],
        compiler_params=_cparams("parallel", "arbitrary"),
        name="rglru",
    )(lx, lg, p["conv_w"], p["conv_b"][None], p["wg"], p["bg"], p["spl"], h0)


def _lru_params(conv_w, conv_b, rw, rb_, iw, ib, lam, cwid):
    d = conv_w.shape[-1]
    ng = d // cwid
    nb = cwid // LRU_BW

    def block_diag(w):
        w = w.reshape(ng, nb, LRU_BW, LRU_BW)
        eye = jnp.eye(nb, dtype=w.dtype)
        return jnp.einsum("gakj,ab->gakbj", w, eye).reshape(ng, cwid, cwid)

    wg = jnp.concatenate([block_diag(rw[0]), block_diag(iw[0]), block_diag(rw[1]), block_diag(iw[1])], axis=-1)
    bg = jnp.concatenate([rb_[0].reshape(ng, cwid), ib[0].reshape(ng, cwid),
                          rb_[1].reshape(ng, cwid), ib[1].reshape(ng, cwid)], axis=-1)[:, None, :]
    spl = jax.nn.softplus(-lam.astype(F32)).reshape(2, ng, cwid).transpose(1, 0, 2)
    return dict(conv_w=conv_w, conv_b=conv_b, wg=wg.astype(BF16), bg=bg, spl=spl)


def _outproj_kernel(route, y_ref, g_ref, x_ref, mod_ref, lnw_ref, wo_ref, n2w_ref, *rest):
    if route:
        rw_ref, x1_ref, h2_ref, rt_ref = rest
    else:
        x1_ref, h2_ref = rest
    d = x_ref.shape[-1]
    m = mod_ref[0]
    gl = (_rms(g_ref[0].astype(F32)) * lnw_ref[...]).astype(BF16)
    dy = y_ref.shape[-1]
    proj = (jnp.dot(y_ref[0], wo_ref[0:dy], preferred_element_type=F32)
            + jnp.dot(gl, wo_ref[dy:], preferred_element_type=F32))
    x1 = x_ref[0] + m[2:3] * proj
    x1_ref[0] = x1
    h2 = _rms(x1) * n2w_ref[...] * (1.0 + m[4:5]) + m[3:4]
    if not route:
        h2_ref[0] = h2.astype(BF16)
        return
    h2_ref[0] = h2
    logits = jnp.dot(h2, rw_ref[...], precision=HI, preferred_element_type=F32)
    lane = lax.broadcasted_iota(jnp.int32, logits.shape, 1).astype(F32)
    neg = -jnp.inf
    logits = jnp.where(lane < N_EXPERTS, logits, neg)
    m1 = jnp.max(logits, axis=-1, keepdims=True)
    i1 = jnp.min(jnp.where(logits == m1, lane, float(LANE)), axis=-1, keepdims=True)
    rest_l = jnp.where(lane == i1, neg, logits)
    m2 = jnp.max(rest_l, axis=-1, keepdims=True)
    i2 = jnp.min(jnp.where(rest_l == m2, lane, float(LANE)), axis=-1, keepdims=True)
    e = jnp.exp(m2 - m1)
    p1 = 1.0 / (1.0 + e)
    p2 = e / (1.0 + e)
    rt_ref[0] = jnp.where(lane == 0.0, i1, jnp.where(lane == 1.0, i2,
                          jnp.where(lane == 2.0, p1, jnp.where(lane == 3.0, p2, 0.0))))


def _outproj(y, g, x, mod_l, mod_row, lru_norm_w, w_out, norm2_w, tm, router_w=None):
    bsz, s, d = x.shape
    route = router_w is not None
    in_specs = [
        pl.BlockSpec((1, tm, y.shape[-1]), lambda b, i: (b, i, 0)),
        pl.BlockSpec((1, tm, g.shape[-1]), lambda b, i: (b, i, 0)),
        pl.BlockSpec((1, tm, d), lambda b, i: (b, i, 0)),
        pl.BlockSpec((1, N_MOD, d), lambda b, i: (mod_row(b), 0, 0)),
        _resident((1, g.shape[-1])),
        _resident(w_out.shape),
        _resident((1, d)),
    ]
    args = [y, g, x, mod_l, lru_norm_w[None], w_out, norm2_w[None]]
    out_specs = [pl.BlockSpec((1, tm, d), lambda b, i: (b, i, 0)), pl.BlockSpec((1, tm, d), lambda b, i: (b, i, 0))]
    out_shape = [jax.ShapeDtypeStruct((bsz, s, d), F32), jax.ShapeDtypeStruct((bsz, s, d), F32 if route else BF16)]
    if route:
        rw = jnp.zeros((d, LANE), F32).at[:, :N_EXPERTS].set(router_w)
        in_specs.append(_resident((d, LANE)))
        args.append(rw)
        out_specs.append(pl.BlockSpec((1, tm, LANE), lambda b, i: (b, i, 0)))
        out_shape.append(jax.ShapeDtypeStruct((bsz, s, LANE), F32))
    return pl.pallas_call(
        functools.partial(_outproj_kernel, route),
        grid=(bsz, s // tm),
        in_specs=in_specs,
        out_specs=out_specs,
        out_shape=out_shape,
        compiler_params=_cparams("parallel", "arbitrary"),
        name="outproj_route" if route else "outproj",
    )(*args)


def _swiglu_acc(h, w1_ref, w3_ref, w2_ref, nf):
    f = w1_ref.shape[-1]
    fc = f // nf
    acc = None
    for j in range(nf):
        a = jnp.dot(h, w1_ref[:, j * fc:(j + 1) * fc], preferred_element_type=F32)
        b = jnp.dot(h, w3_ref[:, j * fc:(j + 1) * fc], preferred_element_type=F32)
        pj = (a * jax.nn.sigmoid(a) * b).astype(BF16)
        o = jnp.dot(pj, w2_ref[j * fc:(j + 1) * fc, :], preferred_element_type=F32)
        acc = o if acc is None else acc + o
    return acc


def _ffn_kernel(final, nf, x1_ref, h2_ref, mod_ref, w1_ref, w3_ref, w2_ref, *rest):
    if final:
        fw_ref, o_ref = rest
    else:
        (o_ref,) = rest
    m = mod_ref[0]
    out = x1_ref[0] + m[5:6] * _swiglu_acc(h2_ref[0], w1_ref, w3_ref, w2_ref, nf)
    if final:
        out = _rms(out) * fw_ref[...]
    o_ref[0] = out


def _ffn(x1, h2, mod_l, mod_row, w1, w3, w2, tm, final_w=None):
    bsz, s, d = x1.shape
    final = final_w is not None
    in_specs = [
        pl.BlockSpec((1, tm, d), lambda b, i: (b, i, 0)),
        pl.BlockSpec((1, tm, d), lambda b, i: (b, i, 0)),
        pl.BlockSpec((1, N_MOD, d), lambda b, i: (mod_row(b), 0, 0)),
        _resident(w1.shape), _resident(w3.shape), _resident(w2.shape),
    ]
    args = [x1, h2, mod_l, w1, w3, w2]
    if final:
        in_specs.append(_resident((1, d)))
        args.append(final_w[None])
    return pl.pallas_call(
        functools.partial(_ffn_kernel, final, 2),
        grid=(bsz, s // tm),
        in_specs=in_specs,
        out_specs=pl.BlockSpec((1, tm, d), lambda b, i: (b, i, 0)),
        out_shape=jax.ShapeDtypeStruct((bsz, s, d), F32),
        compiler_params=_cparams("parallel", "arbitrary"),
        name="ffn_dense",
    )(*args)


def _gather_kernel(tile, idx_ref, src_ref, dst_ref, sem):
    base = pl.program_id(0) * tile

    def issue(r, _):
        pltpu.make_async_copy(src_ref.at[pl.ds(idx_ref[0, 0, r], 1)], dst_ref.at[pl.ds(base + r, 1)], sem).start()
        return 0

    lax.fori_loop(0, tile, issue, 0)

    def drain(r, _):
        pltpu.make_async_copy(src_ref.at[pl.ds(0, 1)], dst_ref.at[pl.ds(base + r, 1)], sem).wait()
        return 0

    lax.fori_loop(0, tile, drain, 0)


def _gather_rows(src, idx, tile):
    n = idx.shape[0]
    return pl.pallas_call(
        functools.partial(_gather_kernel, tile),
        grid=(n // tile,),
        in_specs=[
            pl.BlockSpec((1, 1, tile), lambda i: (i, 0, 0), memory_space=pltpu.SMEM),
            pl.BlockSpec(memory_space=pl.ANY),
        ],
        out_specs=pl.BlockSpec(memory_space=pl.ANY),
        out_shape=jax.ShapeDtypeStruct((n, src.shape[1]), src.dtype),
        scratch_shapes=[pltpu.SemaphoreType.DMA],
        compiler_params=_cparams("arbitrary"),
        name="gather_rows",
    )(idx.reshape(n // tile, 1, tile), src)


def _experts_kernel(nf, te_ref, nu_ref, h_ref, w1_ref, w3_ref, w2_ref, o_ref):
    @pl.when(pl.program_id(0) < nu_ref[0])
    def _():
        o_ref[...] = _swiglu_acc(h_ref[...].astype(BF16), w1_ref.at[0], w3_ref.at[0], w2_ref.at[0], nf)

    @pl.when(pl.program_id(0) >= nu_ref[0])
    def _():
        o_ref[...] = jnp.zeros_like(o_ref)


def _experts(h_sorted, tile_expert, n_used, w1, w3, w2, tile):
    n, d = h_sorted.shape
    f = w1.shape[-1]
    grid_spec = pltpu.PrefetchScalarGridSpec(
        num_scalar_prefetch=2,
        grid=(n // tile,),
        in_specs=[
            pl.BlockSpec((tile, d), lambda i, te, nu: (i, 0)),
            pl.BlockSpec((1, d, f), lambda i, te, nu: (te[i], 0, 0), pipeline_mode=pl.Buffered(1)),
            pl.BlockSpec((1, d, f), lambda i, te, nu: (te[i], 0, 0), pipeline_mode=pl.Buffered(1)),
            pl.BlockSpec((1, f, d), lambda i, te, nu: (te[i], 0, 0), pipeline_mode=pl.Buffered(1)),
        ],
        out_specs=pl.BlockSpec((tile, d), lambda i, te, nu: (i, 0)),
    )
    return pl.pallas_call(
        functools.partial(_experts_kernel, 2),
        grid_spec=grid_spec,
        out_shape=jax.ShapeDtypeStruct((n, d), F32),
        compiler_params=_cparams("arbitrary"),
        name="experts",
    )(tile_expert, n_used, h_sorted, w1, w3, w2)


def _combine_kernel(x1_ref, ya_ref, yb_ref, rt_ref, mod_ref, fw_ref, o_ref):
    m = mod_ref[0]
    rt = rt_ref[0]
    moe = rt[:, 2:3] * ya_ref[0] + rt[:, 3:4] * yb_ref[0]
    o_ref[0] = _rms(x1_ref[0] + m[5:6] * moe) * fw_ref[...]


def _moe(x1, h2, route, mod_l, mod_row, w1, w3, w2, final_w, tm, tile):
    bsz, s, d = x1.shape
    n = bsz * s
    ex = route[..., :2].astype(jnp.int32).reshape(2 * n)
    onehot = (ex[:, None] == jnp.arange(N_EXPERTS, dtype=jnp.int32)[None, :]).astype(jnp.int32)
    csum = jnp.cumsum(onehot, axis=0)
    cnt = csum[-1]
    rank = jnp.sum((csum - onehot) * onehot, axis=1)
    padded = ((cnt + tile - 1) // tile) * tile
    ends = jnp.cumsum(padded)
    pos = (ends - padded)[ex] + rank
    n_rows = 2 * n + N_EXPERTS * tile
    src = jnp.zeros((n_rows,), jnp.int32).at[pos].set(jnp.arange(2 * n, dtype=jnp.int32) // 2)
    tile_start = jnp.arange(n_rows // tile, dtype=jnp.int32) * tile
    tile_expert = jnp.minimum(jnp.searchsorted(ends, tile_start, side="right"), N_EXPERTS - 1).astype(jnp.int32)
    n_used = (ends[-1] // tile).astype(jnp.int32).reshape(1)

    h_sorted = _gather_rows(h2.reshape(n, d), src, tile)
    y_sorted = _experts(h_sorted, tile_expert, n_used, w1, w3, w2, tile)
    pos2 = pos.reshape(n, 2)
    ya = _gather_rows(y_sorted, pos2[:, 0], tile).reshape(bsz, s, d)
    yb = _gather_rows(y_sorted, pos2[:, 1], tile).reshape(bsz, s, d)
    tok = pl.BlockSpec((1, tm, d), lambda b, i: (b, i, 0))
    return pl.pallas_call(
        _combine_kernel,
        grid=(bsz, s // tm),
        in_specs=[tok, tok, tok,
                  pl.BlockSpec((1, tm, LANE), lambda b, i: (b, i, 0)),
                  pl.BlockSpec((1, N_MOD, d), lambda b, i: (mod_row(b), 0, 0)),
                  _resident((1, d))],
        out_specs=tok,
        out_shape=jax.ShapeDtypeStruct((bsz, s, d), F32),
        compiler_params=_cparams("parallel", "arbitrary"),
        name="moe_combine",
    )(x1, ya, yb, route, mod_l, final_w[None])


def _pick_tile(s, pref):
    t = min(s, pref)
    while s % t:
        t //= 2
    return t


def kernel(x, c, ctx, c_ctx, mod_w, mod_b, norm1_w, norm2_w, w_in, ssd_conv_w, ssd_conv_b, ssd_dt_bias,
           ssd_a_log, ssd_d, ssd_norm_w, lru_conv_w, lru_conv_b, lru_rw, lru_rb, lru_iw, lru_ib, lru_lambda,
           lru_norm_w, w_out, ffn_w1, ffn_w3, ffn_w2, router_w, moe_w1, moe_w3, moe_w2, final_norm_w):
    depth = w_in.shape[0]
    bsz, seq, d = x.shape
    lc = ctx.shape[1]
    rows = seq // GRID_W
    heads = ssd_dt_bias.shape[-1]
    d_ssd = heads * SSD_HEADDIM
    d_xbc = ssd_conv_w.shape[-1]
    d_lru = lru_conv_w.shape[-1]
    splits = (d_ssd, d_xbc, d_lru, d_lru)
    off_dt = d_ssd + d_xbc
    off_lx = off_dt + 2 * heads
    perm = jnp.concatenate([jnp.arange(0, off_dt), jnp.arange(off_lx, off_lx + 2 * d_lru),
                            jnp.arange(off_dt, off_lx)])
    ctx_cols = 8
    ctx_rows = lc // ctx_cols
    lru_cw = 256

    mod = _modulation(c, c_ctx, mod_w, mod_b)
    lat_row = lambda b: b
    ctx_row = lambda b: bsz
    tm_lat = _pick_tile(seq, 512)
    tm_ctx = _pick_tile(lc, 512)
    tb_lat = _pick_tile(seq, 512)
    tb_ctx = _pick_tile(lc, 512)
    gw = d_ssd // SSD_GROUPS

    xc = ctx
    for l in range(depth):
        last = l == depth - 1
        mod_l = mod[l]
        w_perm = jnp.pad(w_in[l][:, perm], ((0, 0), (0, LANE - 2 * heads))).astype(BF16)
        ssd_p = dict(conv_w=ssd_conv_w[l], conv_b=ssd_conv_b[l], dt_bias=ssd_dt_bias[l], a_log=ssd_a_log[l],
                     d_skip=ssd_d[l], norm_w=ssd_norm_w[l])
        lru_p = _lru_params(lru_conv_w[l], lru_conv_b[l], lru_rw[l], lru_rb[l], lru_iw[l], lru_ib[l],
                            lru_lambda[l], lru_cw)
        wo = w_out[l].astype(BF16)

        zc, xbcc, lxc, lgc, dtc = _inproj(xc, mod_l, ctx_row, norm1_w[l], w_perm, splits, 2 * heads, tm_ctx)
        zl, xbcl, lxl, lgl, dtl = _inproj(x, mod_l, lat_row, norm1_w[l], w_perm, splits, 2 * heads, tm_lat)

        zero_s = jnp.zeros((bsz, SSD_GROUPS, SSD_STATE, gw), F32)
        yc_ssd, s_f, s_b = _ssd_bidir(zc, xbcc, dtc, ssd_p, zero_s, zero_s, tb_ctx)
        yl_ssd, _, _ = _ssd_bidir(zl, xbcl, dtl, ssd_p, s_f, s_b, tb_lat)

        to_grid = lambda v: v.reshape(bsz, ctx_cols, ctx_rows, d_lru).transpose(0, 2, 1, 3)
        gc, h_fin = _lru(to_grid(lxc), to_grid(lgc), lru_p, jnp.zeros((bsz, 2, d_lru), F32), ctx_rows, True)
        gl, _ = _lru(lxl.reshape(bsz, rows, GRID_W, d_lru), lgl.reshape(bsz, rows, GRID_W, d_lru), lru_p,
                     h_fin, 8, False)
        gl = gl.reshape(bsz, seq, d_lru)

        if l % 2 == 0:
            w1, w3, w2 = (ffn_w1[l // 2].astype(BF16), ffn_w3[l // 2].astype(BF16), ffn_w2[l // 2].astype(BF16))
            x1, h2 = _outproj(yl_ssd, gl, x, mod_l, lat_row, lru_norm_w[l], wo, norm2_w[l], tm_lat)
            x = _ffn(x1, h2, mod_l, lat_row, w1, w3, w2, tm_lat, final_norm_w if last else None)
            if not last:
                gc = gc.transpose(0, 2, 1, 3).reshape(bsz, lc, d_lru)
                xc1, hc2 = _outproj(yc_ssd, gc, xc, mod_l, ctx_row, lru_norm_w[l], wo, norm2_w[l], tm_ctx)
                xc = _ffn(xc1, hc2, mod_l, ctx_row, w1, w3, w2, tm_ctx)
        else:
            w1, w3, w2 = (moe_w1[l // 2].astype(BF16), moe_w3[l // 2].astype(BF16), moe_w2[l // 2].astype(BF16))
            assert last, "a routed layer that is not the last layer is not implemented"
            x1, h2, route = _outproj(yl_ssd, gl, x, mod_l, lat_row, lru_norm_w[l], wo, norm2_w[l], tm_lat,
                                     router_w=router_w[l // 2])
            x = _moe(x1, h2, route, mod_l, lat_row, w1, w3, w2, final_norm_w, tm_lat, 512)
    return x
```

```python
import functools

import jax
import jax.numpy as jnp
from jax import lax
from jax.experimental import pallas as pl
from jax.experimental.pallas import tpu as pltpu
from jax.experimental.pallas import tpu_sc as plsc

F32 = jnp.float32
BF16 = jnp.bfloat16
HI = lax.Precision.HIGHEST

EPS = 1e-6
N_MOD = 6
GRID_W = 64
SSD_HEADDIM = 64
SSD_GROUPS = 2
SSD_STATE = 128
SSD_CHUNK = 128
CONV_K = 4
LRU_BW = 64
RGLRU_C = 8.0
N_EXPERTS = 8
LANE = 128
MOD_ROWS = 24
VMEM_LIMIT = 56 * 1024 * 1024
SC_CORES = 2
SC_WORKERS = 32
SC_WINDOW = 32


def _cparams(*sem):
    return pltpu.CompilerParams(dimension_semantics=sem, vmem_limit_bytes=VMEM_LIMIT)


def _resident(shape):
    nd = len(shape)
    return pl.BlockSpec(shape, lambda *_: (0,) * nd, pipeline_mode=pl.Buffered(1))


def _rms(x):
    return x * lax.rsqrt(jnp.mean(x * x, axis=-1, keepdims=True) + EPS)


def _mod_kernel(s_ref, w_ref, b_ref, o_ref):
    s = s_ref[...]
    s = s * jax.nn.sigmoid(s)
    o_ref[0] = jnp.dot(s, w_ref[0], precision=HI, preferred_element_type=F32) + b_ref[0]


def _modulation(c, c_ctx, mod_w, mod_b):
    depth, d, n = mod_w.shape
    bsz = c.shape[0]
    assert bsz + 1 <= MOD_ROWS
    s = jnp.concatenate([c, c_ctx[None], jnp.zeros((MOD_ROWS - bsz - 1, d), F32)], axis=0)
    tn = n // 4
    out = pl.pallas_call(
        _mod_kernel,
        grid=(depth, n // tn),
        in_specs=[
            pl.BlockSpec((MOD_ROWS, d), lambda l, j: (0, 0)),
            pl.BlockSpec((1, d, tn), lambda l, j: (l, 0, j)),
            pl.BlockSpec((1, 1, tn), lambda l, j: (l, 0, j)),
        ],
        out_specs=pl.BlockSpec((1, MOD_ROWS, tn), lambda l, j: (l, 0, j)),
        out_shape=jax.ShapeDtypeStruct((depth, MOD_ROWS, n), F32),
        compiler_params=_cparams("arbitrary", "arbitrary"),
        name="modulation",
    )(s, mod_w, mod_b[:, None, :])
    return out.reshape(depth, MOD_ROWS, N_MOD, d)


def _inproj_kernel(splits, dt_w, x_ref, mod_ref, nw_ref, w_ref, z_ref, xbc_ref, lx_ref, lg_ref, dt_ref):
    m = mod_ref[0]
    h = (_rms(x_ref[0]) * nw_ref[...] * (1.0 + m[1:2]) + m[0:1]).astype(BF16)

    def proj(a, b):
        return jnp.dot(h, w_ref[:, a:b], preferred_element_type=F32)

    o = [0]
    for s in splits:
        o.append(o[-1] + s)
    z_ref[0] = proj(o[0], o[1]).astype(BF16)
    xbc_ref[0] = proj(o[1], o[2]).astype(BF16)
    lx_ref[0] = proj(o[2], o[3]).astype(BF16)
    lg_ref[0] = proj(o[3], o[4]).astype(BF16)
    dt_ref[0] = proj(o[4], o[4] + LANE)[:, :dt_w]


def _inproj(x, mod_l, mod_row, norm_w, w_perm, splits, dt_w, tm):
    bsz, s, d = x.shape
    n_all = w_perm.shape[1]
    outs = [jax.ShapeDtypeStruct((bsz, s, w), BF16) for w in splits] + [jax.ShapeDtypeStruct((bsz, s, dt_w), F32)]
    return pl.pallas_call(
        functools.partial(_inproj_kernel, splits, dt_w),
        grid=(bsz, s // tm),
        in_specs=[
            pl.BlockSpec((1, tm, d), lambda b, i: (b, i, 0)),
            pl.BlockSpec((1, N_MOD, d), lambda b, i: (mod_row(b), 0, 0)),
            _resident((1, d)),
            _resident((d, n_all)),
        ],
        out_specs=[pl.BlockSpec((1, tm, w), lambda b, i: (b, i, 0)) for w in splits]
        + [pl.BlockSpec((1, tm, dt_w), lambda b, i: (b, i, 0))],
        out_shape=outs,
        compiler_params=_cparams("parallel", "arbitrary"),
        name="inproj",
    )(x, mod_l, norm_w[None], w_perm)


def _softplus(x):
    return jnp.maximum(x, 0.0) + jnp.log1p(jnp.exp(-jnp.abs(x)))


def _ssd_kernel(rev, nblk, tb, heads, *refs):
    if rev:
        (xbc_ref, xprev_ref, xnext_ref, dt_ref, cw_ref, cb_ref, dtb_ref, alog_ref, s0_ref,
         yf_ref, z_ref, dsk_ref, nw_ref, y_ref, sfin_ref, xp_ref, st_ref, yb_ref) = refs
    else:
        (xbc_ref, xprev_ref, xnext_ref, dt_ref, cw_ref, cb_ref, dtb_ref, alog_ref, s0_ref,
         y_ref, sfin_ref, xp_ref, st_ref) = refs
        yb_ref = y_ref.at[0]
    i = pl.program_id(1)
    blk = (nblk - 1 - i) if rev else i
    d_ssd = heads * SSD_HEADDIM
    gn = SSD_GROUPS * SSD_STATE
    gw = d_ssd // SSD_GROUPS
    ck = SSD_CHUNK

    @pl.when(i == 0)
    def _():
        st_ref[...] = s0_ref[0]

    xp_ref[8:8 + tb] = xbc_ref[0].astype(F32)
    xp_ref[0:8] = jnp.where(blk > 0, xprev_ref[0, 0].astype(F32), 0.0)
    xp_ref[8 + tb:16 + tb] = jnp.where(blk < nblk - 1, xnext_ref[0, 0].astype(F32), 0.0)
    acc = cb_ref[...] + cw_ref[0:1] * xp_ref[6:6 + tb]
    for k in range(1, CONV_K):
        acc = acc + cw_ref[k:k + 1] * xp_ref[6 + k:6 + k + tb]
    xs = acc * jax.nn.sigmoid(acc)

    d0 = heads if rev else 0
    dt = _softplus(dt_ref[0] + dtb_ref[...])[:, d0:d0 + heads]
    a_neg = -jnp.exp(alog_ref[...])[:, d0:d0 + heads]
    da = dt * a_neg
    ldt = jnp.where(dt > 0.0, jnp.log(dt), -1e30)

    ri = lax.broadcasted_iota(jnp.int32, (ck, ck), 0)
    ci = lax.broadcasted_iota(jnp.int32, (ck, ck), 1)
    tri = (ci >= ri) if rev else (ci <= ri)
    tri_f = tri.astype(F32)
    tri_t = ((ri >= ci) if rev else (ri <= ci)).astype(F32)
    eye = (ri == ci).astype(F32)
    rhs_rt = jnp.concatenate([tri_t, eye], axis=0)
    eh = lax.broadcasted_iota(jnp.int32, (heads, d_ssd), 0)
    ec = lax.broadcasted_iota(jnp.int32, (heads, d_ssd), 1)
    expand = jnp.where(ec // SSD_HEADDIM == eh, 1.0, 0.0).astype(BF16)
    expand2 = jnp.concatenate([expand, expand], axis=0)
    lane = lax.broadcasted_iota(jnp.int32, (ck, LANE), 1)
    end = 0 if rev else ck - 1

    def widen(q):
        hi = q.astype(BF16)
        lo = (q - hi.astype(F32)).astype(BF16)
        return jnp.dot(jnp.concatenate([hi, lo], axis=1), expand2, preferred_element_type=F32)

    order = range(tb // ck - 1, -1, -1) if rev else range(tb // ck)
    for c in order:
        r0 = c * ck
        xc = xs[r0:r0 + ck]
        x_bf = xc[:, :d_ssd].astype(BF16)
        b_bf = xc[:, d_ssd:d_ssd + gn].astype(BF16)
        c_bf = xc[:, d_ssd + gn:].astype(BF16)
        dac = da[r0:r0 + ck]
        cs = jnp.dot(tri_f, dac, precision=HI, preferred_element_type=F32)
        r_t = lax.dot_general(jnp.concatenate([dac, -ldt[r0:r0 + ck]], axis=0), rhs_rt,
                              (((0,), (0,)), ((), ())), precision=HI,
                              preferred_element_type=F32)
        ecs = jnp.exp(cs)
        tot = cs[end:end + 1]
        wgt = jnp.exp(tot - cs) * dt[r0:r0 + ck]
        ecs_w = widen(ecs)
        xw = (xc[:, :d_ssd] * widen(wgt)).astype(BF16)
        for g in range(SSD_GROUPS):
            bg = b_bf[:, g * SSD_STATE:(g + 1) * SSD_STATE]
            cg = c_bf[:, g * SSD_STATE:(g + 1) * SSD_STATE]
            cb = lax.dot_general(cg, bg, (((1,), (1,)), ((), ())), preferred_element_type=F32)
            s_in = st_ref[g]
            y_off = jnp.dot(cg, s_in.astype(BF16), preferred_element_type=F32) * ecs_w[:, g * gw:(g + 1) * gw]
            new = lax.dot_general(bg, xw[:, g * gw:(g + 1) * gw], (((0,), (0,)), ((), ())),
                                  preferred_element_type=F32)
            st_ref[g] = ecs_w[end:end + 1, g * gw:(g + 1) * gw] * s_in + new
            for p in range(gw // LANE):
                ha = (g * gw + p * LANE) // SSD_HEADDIM
                lm = []
                for h in (ha, ha + 1):
                    seg = cs[:, h:h + 1] - r_t[h:h + 1, :]
                    lm.append(jnp.where(tri, jnp.exp(seg), 0.0) * cb)
                lhs = jnp.concatenate(lm, axis=1).astype(BF16)
                xpair = x_bf[:, g * gw + p * LANE:g * gw + (p + 1) * LANE]
                zero = jnp.zeros_like(xpair)
                rhs = jnp.concatenate([jnp.where(lane < SSD_HEADDIM, xpair, zero),
                                       jnp.where(lane >= SSD_HEADDIM, xpair, zero)], axis=0)
                col = g * gw + p * LANE
                yb_ref[r0:r0 + ck, col:col + LANE] = (
                    jnp.dot(lhs, rhs, preferred_element_type=F32) + y_off[:, p * LANE:(p + 1) * LANE])

    if rev:
        z = z_ref[0].astype(F32)
        tot_y = (yb_ref[...] + yf_ref[0] + dsk_ref[...] * xs[:, :d_ssd]) * (z * jax.nn.sigmoid(z))
        y_ref[0] = (_rms(tot_y) * nw_ref[...]).astype(BF16)

    @pl.when(i == nblk - 1)
    def _():
        sfin_ref[0] = st_ref[...]


def _ssd_dir(rev, xbc, dt_raw, conv_w, conv_b, dt_bias, a_log, s0, tb, extra=None):
    bsz, s, dxbc = xbc.shape
    heads = dt_raw.shape[-1] // 2
    d_ssd = heads * SSD_HEADDIM
    gw = d_ssd // SSD_GROUPS
    nblk = s // tb
    nb8 = tb // 8
    xbc8 = xbc.reshape(bsz, s // 8, 8, dxbc)
    pos = (lambda i: nblk - 1 - i) if rev else (lambda i: i)
    in_specs = [
        pl.BlockSpec((1, tb, dxbc), lambda b, i: (b, pos(i), 0)),
        pl.BlockSpec((1, 1, 8, dxbc), lambda b, i: (b, jnp.maximum(pos(i) * nb8 - 1, 0), 0, 0)),
        pl.BlockSpec((1, 1, 8, dxbc), lambda b, i: (b, jnp.minimum((pos(i) + 1) * nb8, s // 8 - 1), 0, 0)),
        pl.BlockSpec((1, tb, 2 * heads), lambda b, i: (b, pos(i), 0)),
        _resident((CONV_K, dxbc)),
        _resident((1, dxbc)),
        _resident((1, 2 * heads)),
        _resident((1, 2 * heads)),
        pl.BlockSpec((1, SSD_GROUPS, SSD_STATE, gw), lambda b, i: (b, 0, 0, 0)),
    ]
    args = [xbc, xbc8, xbc8, dt_raw, conv_w, conv_b[None], dt_bias.reshape(1, -1), a_log.reshape(1, -1), s0]
    st_shape = jax.ShapeDtypeStruct((bsz, SSD_GROUPS, SSD_STATE, gw), F32)
    st_spec = pl.BlockSpec((1, SSD_GROUPS, SSD_STATE, gw), lambda b, i: (b, 0, 0, 0))
    scratch = [pltpu.VMEM((tb + 16, dxbc), F32), pltpu.VMEM((SSD_GROUPS, SSD_STATE, gw), F32)]
    if rev:
        yf, z, d_skip, norm_w = extra
        in_specs += [
            pl.BlockSpec((1, tb, d_ssd), lambda b, i: (b, pos(i), 0)),
            pl.BlockSpec((1, tb, d_ssd), lambda b, i: (b, pos(i), 0)),
            _resident((1, d_ssd)),
            _resident((1, d_ssd)),
        ]
        args += [yf, z, jnp.repeat(d_skip, SSD_HEADDIM)[None], norm_w[None]]
        y_dtype = BF16
        scratch.append(pltpu.VMEM((tb, d_ssd), F32))
    else:
        y_dtype = F32
    return pl.pallas_call(
        functools.partial(_ssd_kernel, rev, nblk, tb, heads),
        grid=(bsz, nblk),
        in_specs=in_specs,
        out_specs=[pl.BlockSpec((1, tb, d_ssd), lambda b, i: (b, pos(i), 0)), st_spec],
        out_shape=[jax.ShapeDtypeStruct((bsz, s, d_ssd), y_dtype), st_shape],
        scratch_shapes=scratch,
        compiler_params=_cparams("parallel", "arbitrary"),
        name="ssd_bwd" if rev else "ssd_fwd",
    )(*args)


def _ssd_bidir(z, xbc, dt_raw, p, s0_f, s0_b, tb):
    yf, s_f = _ssd_dir(False, xbc, dt_raw, p["conv_w"], p["conv_b"], p["dt_bias"], p["a_log"], s0_f, tb)
    y, s_b = _ssd_dir(True, xbc, dt_raw, p["conv_w"], p["conv_b"], p["dt_bias"], p["a_log"], s0_b, tb,
                      extra=(yf, z, p["d_skip"], p["norm_w"]))
    return y, s_f, s_b


def _shift_rows(x, s, fill, up):
    c = x.shape[0]
    rolled = pltpu.roll(x, (c - s) if up else s, 0)
    row = lax.broadcasted_iota(jnp.int32, x.shape, 0)
    keep = (row < c - s) if up else (row >= s)
    return jnp.where(keep, rolled, fill)


def _lru_kernel(nr, nc, rb, continuous, lx_ref, lg_ref, cw_ref, cb_ref, wg_ref, bg_ref, spl_ref, h0_ref,
                g_ref, hfin_ref, xp_ref, a_ref, b_ref):
    cwid = lx_ref.shape[-1]
    xin = lx_ref[0].astype(F32)
    xp_ref[2:nr + 2] = xin
    if continuous:
        for k in range(2):
            xp_ref[k] = _shift_rows(xin[nr - 2 + k], 1, 0.0, up=False)
        xp_ref[nr + 2] = _shift_rows(xin[0], 1, 0.0, up=True)
    else:
        xp_ref[0:2] = jnp.zeros((2, nc, cwid), F32)
        xp_ref[nr + 2:nr + 3] = jnp.zeros((1, nc, cwid), F32)

    def gates(t, _):
        r0 = pl.multiple_of(t * rb, rb)
        xc = cb_ref[...] + cw_ref[0:1] * xp_ref[pl.ds(r0, rb)]
        for k in range(1, CONV_K):
            xc = xc + cw_ref[k:k + 1] * xp_ref[pl.ds(r0 + k, rb)]
        xc2 = xc.reshape(rb * nc, cwid)
        gt = jnp.dot(xc2.astype(BF16), wg_ref[0], preferred_element_type=F32) + bg_ref[0]
        for d in range(2):
            r = jax.nn.sigmoid(gt[:, (2 * d) * cwid:(2 * d + 1) * cwid])
            ig = jax.nn.sigmoid(gt[:, (2 * d + 1) * cwid:(2 * d + 2) * cwid])
            log_a = -RGLRU_C * r * spl_ref[0, d:d + 1]
            a = jnp.exp(log_a)
            one_m_a2 = -jnp.tanh(log_a) * (a * a + 1.0)
            a_ref[d, pl.ds(r0, rb)] = a.reshape(rb, nc, cwid)
            b_ref[d, pl.ds(r0, rb)] = (jnp.sqrt(one_m_a2) * ig * xc2).reshape(rb, nc, cwid)
        return 0

    lax.fori_loop(0, nr // rb, gates, 0)

    hin = []
    for d in range(2):
        def step(t, carry):
            h, p = carry
            r = (nr - 1 - t) if d else t
            a = a_ref[d, r]
            h = a * h + b_ref[d, r]
            p = a * p
            b_ref[d, r] = h
            a_ref[d, r] = p
            return h, p

        h_l, p_l = lax.fori_loop(0, nr, step, (jnp.zeros((nc, cwid), F32), jnp.ones((nc, cwid), F32)))
        sa, sb = p_l, h_l
        s = 1
        while s < nc:
            sb = sa * _shift_rows(sb, s, 0.0, up=bool(d)) + sb
            sa = sa * _shift_rows(sa, s, 1.0, up=bool(d))
            s *= 2
        h0 = h0_ref[0, d:d + 1]
        after = sa * h0 + sb
        hin.append(_shift_rows(after, 1, h0, up=bool(d)) if nc > 1 else jnp.broadcast_to(h0, (nc, cwid)))
        last = 0 if d else nc - 1
        hfin_ref[0, d:d + 1] = after[last:last + 1]

    def finish(t, _):
        r0 = pl.multiple_of(t * rb, rb)
        h = (b_ref[0, pl.ds(r0, rb)] + a_ref[0, pl.ds(r0, rb)] * hin[0]
             + b_ref[1, pl.ds(r0, rb)] + a_ref[1, pl.ds(r0, rb)] * hin[1])
        g_ref[0, pl.ds(r0, rb)] = (h * jax.nn.gelu(lg_ref[0, pl.ds(r0, rb)].astype(F32))).astype(BF16)
        return 0

    lax.fori_loop(0, nr // rb, finish, 0)


def _lru(lx, lg, p, h0, rb, continuous):
    bsz, nr, nc, d = lx.shape
    cwid = p["wg"].shape[1]
    ng = d // cwid
    return pl.pallas_call(
        functools.partial(_lru_kernel, nr, nc, rb, continuous),
        grid=(bsz, ng),
        in_specs=[
            pl.BlockSpec((1, nr, nc, cwid), lambda b, j: (b, 0, 0, j)),
            pl.BlockSpec((1, nr, nc, cwid), lambda b, j: (b, 0, 0, j)),
            pl.BlockSpec((CONV_K, cwid), lambda b, j: (0, j)),
            pl.BlockSpec((1, cwid), lambda b, j: (0, j)),
            pl.BlockSpec((1, cwid, 4 * cwid), lambda b, j: (j, 0, 0)),
            pl.BlockSpec((1, 1, 4 * cwid), lambda b, j: (j, 0, 0)),
            pl.BlockSpec((1, 2, cwid), lambda b, j: (j, 0, 0)),
            pl.BlockSpec((1, 2, cwid), lambda b, j: (b, 0, j)),
        ],
        out_specs=[
            pl.BlockSpec((1, nr, nc, cwid), lambda b, j: (b, 0, 0, j)),
            pl.BlockSpec((1, 2, cwid), lambda b, j: (b, 0, j)),
        ],
        out_shape=[jax.ShapeDtypeStruct((bsz, nr, nc, d), BF16), jax.ShapeDtypeStruct((bsz, 2, d), F32)],
        scratch_shapes=[
            pltpu.VMEM((nr + 3, nc, cwid), F32),
            pltpu.VMEM((2, nr, nc, cwid), F32),
            pltpu.VMEM((2, nr, nc, cwid), F32),
        ],
        compiler_params=_cparams("parallel", "arbitrary"),
        name="rglru",
    )(lx, lg, p["conv_w"], p["conv_b"][None], p["wg"], p["bg"], p["spl"], h0)


def _lru_params(conv_w, conv_b, rw, rb_, iw, ib, lam, cwid):
    d = conv_w.shape[-1]
    ng = d // cwid
    nb = cwid // LRU_BW

    def block_diag(w):
        w = w.reshape(ng, nb, LRU_BW, LRU_BW)
        eye = jnp.eye(nb, dtype=w.dtype)
        return jnp.einsum("gakj,ab->gakbj", w, eye).reshape(ng, cwid, cwid)

    wg = jnp.concatenate([block_diag(rw[0]), block_diag(iw[0]), block_diag(rw[1]), block_diag(iw[1])], axis=-1)
    bg = jnp.concatenate([rb_[0].reshape(ng, cwid), ib[0].reshape(ng, cwid),
                          rb_[1].reshape(ng, cwid), ib[1].reshape(ng, cwid)], axis=-1)[:, None, :]
    spl = jax.nn.softplus(-lam.astype(F32)).reshape(2, ng, cwid).transpose(1, 0, 2)
    return dict(conv_w=conv_w, conv_b=conv_b, wg=wg.astype(BF16), bg=bg, spl=spl)


def _outproj_kernel(route, y_ref, g_ref, x_ref, mod_ref, lnw_ref, wo_ref, n2w_ref, *rest):
    if route:
        rw_ref, x1_ref, h2_ref, rt_ref = rest
    else:
        x1_ref, h2_ref = rest
    d = x_ref.shape[-1]
    m = mod_ref[0]
    gl = (_rms(g_ref[0].astype(F32)) * lnw_ref[...]).astype(BF16)
    dy = y_ref.shape[-1]
    proj = (jnp.dot(y_ref[0], wo_ref[0:dy], preferred_element_type=F32)
            + jnp.dot(gl, wo_ref[dy:], preferred_element_type=F32))
    x1 = x_ref[0] + m[2:3] * proj
    x1_ref[0] = x1
    h2 = _rms(x1) * n2w_ref[...] * (1.0 + m[4:5]) + m[3:4]
    if not route:
        h2_ref[0] = h2.astype(BF16)
        return
    h2_ref[0] = h2
    h_hi = h2.astype(BF16)
    h_lo = (h2 - h_hi.astype(F32)).astype(BF16)
    r_hi = jnp.dot(h_hi, rw_ref[...], preferred_element_type=F32)
    logits = (r_hi[:, :LANE] + r_hi[:, LANE:]
              + jnp.dot(h_lo, rw_ref[:, :LANE], preferred_element_type=F32))
    lane = lax.broadcasted_iota(jnp.int32, logits.shape, 1).astype(F32)
    neg = -jnp.inf
    logits = jnp.where(lane < N_EXPERTS, logits, neg)
    m1 = jnp.max(logits, axis=-1, keepdims=True)
    i1 = jnp.min(jnp.where(logits == m1, lane, float(LANE)), axis=-1, keepdims=True)
    rest_l = jnp.where(lane == i1, neg, logits)
    m2 = jnp.max(rest_l, axis=-1, keepdims=True)
    i2 = jnp.min(jnp.where(rest_l == m2, lane, float(LANE)), axis=-1, keepdims=True)
    e = jnp.exp(m2 - m1)
    p1 = 1.0 / (1.0 + e)
    p2 = e / (1.0 + e)
    rt_ref[0] = jnp.where(lane == 0.0, i1, jnp.where(lane == 1.0, i2,
                          jnp.where(lane == 2.0, p1, jnp.where(lane == 3.0, p2, 0.0))))


def _outproj(y, g, x, mod_l, mod_row, lru_norm_w, w_out, norm2_w, tm, router_w=None):
    bsz, s, d = x.shape
    route = router_w is not None
    in_specs = [
        pl.BlockSpec((1, tm, y.shape[-1]), lambda b, i: (b, i, 0)),
        pl.BlockSpec((1, tm, g.shape[-1]), lambda b, i: (b, i, 0)),
        pl.BlockSpec((1, tm, d), lambda b, i: (b, i, 0)),
        pl.BlockSpec((1, N_MOD, d), lambda b, i: (mod_row(b), 0, 0)),
        _resident((1, g.shape[-1])),
        _resident(w_out.shape),
        _resident((1, d)),
    ]
    args = [y, g, x, mod_l, lru_norm_w[None], w_out, norm2_w[None]]
    out_specs = [pl.BlockSpec((1, tm, d), lambda b, i: (b, i, 0)), pl.BlockSpec((1, tm, d), lambda b, i: (b, i, 0))]
    out_shape = [jax.ShapeDtypeStruct((bsz, s, d), F32), jax.ShapeDtypeStruct((bsz, s, d), F32 if route else BF16)]
    if route:
        rw = jnp.zeros((d, LANE), F32).at[:, :N_EXPERTS].set(router_w)
        rw_hi = rw.astype(BF16)
        in_specs.append(_resident((d, 2 * LANE)))
        args.append(jnp.concatenate([rw_hi, (rw - rw_hi.astype(F32)).astype(BF16)], axis=1))
        out_specs.append(pl.BlockSpec((1, tm, LANE), lambda b, i: (b, i, 0)))
        out_shape.append(jax.ShapeDtypeStruct((bsz, s, LANE), F32))
    return pl.pallas_call(
        functools.partial(_outproj_kernel, route),
        grid=(bsz, s // tm),
        in_specs=in_specs,
        out_specs=out_specs,
        out_shape=out_shape,
        compiler_params=_cparams("parallel", "arbitrary"),
        name="outproj_route" if route else "outproj",
    )(*args)


def _swiglu_acc(h, w1_ref, w3_ref, w2_ref, nf):
    f = w1_ref.shape[-1]
    fc = f // nf
    acc = None
    for j in range(nf):
        a = jnp.dot(h, w1_ref[:, j * fc:(j + 1) * fc], preferred_element_type=F32)
        b = jnp.dot(h, w3_ref[:, j * fc:(j + 1) * fc], preferred_element_type=F32)
        pj = (a * jax.nn.sigmoid(a) * b).astype(BF16)
        o = jnp.dot(pj, w2_ref[j * fc:(j + 1) * fc, :], preferred_element_type=F32)
        acc = o if acc is None else acc + o
    return acc


def _ffn_kernel(final, nf, x1_ref, h2_ref, mod_ref, w1_ref, w3_ref, w2_ref, *rest):
    if final:
        fw_ref, o_ref = rest
    else:
        (o_ref,) = rest
    m = mod_ref[0]
    out = x1_ref[0] + m[5:6] * _swiglu_acc(h2_ref[0], w1_ref, w3_ref, w2_ref, nf)
    if final:
        out = _rms(out) * fw_ref[...]
    o_ref[0] = out


def _ffn(x1, h2, mod_l, mod_row, w1, w3, w2, tm, final_w=None):
    bsz, s, d = x1.shape
    final = final_w is not None
    in_specs = [
        pl.BlockSpec((1, tm, d), lambda b, i: (b, i, 0)),
        pl.BlockSpec((1, tm, d), lambda b, i: (b, i, 0)),
        pl.BlockSpec((1, N_MOD, d), lambda b, i: (mod_row(b), 0, 0)),
        _resident(w1.shape), _resident(w3.shape), _resident(w2.shape),
    ]
    args = [x1, h2, mod_l, w1, w3, w2]
    if final:
        in_specs.append(_resident((1, d)))
        args.append(final_w[None])
    return pl.pallas_call(
        functools.partial(_ffn_kernel, final, 2),
        grid=(bsz, s // tm),
        in_specs=in_specs,
        out_specs=pl.BlockSpec((1, tm, d), lambda b, i: (b, i, 0)),
        out_shape=jax.ShapeDtypeStruct((bsz, s, d), F32),
        compiler_params=_cparams("parallel", "arbitrary"),
        name="ffn_dense",
    )(*args)


def _sc_worker_id():
    return lax.axis_index("subcore") * SC_CORES + lax.axis_index("core")


def _sc_mesh():
    return plsc.VectorSubcoreMesh(core_axis_name="core", subcore_axis_name="subcore")


def _sc_scatter2(x, i0, i1, n_rows):
    m, d = x.shape
    per_w = m // SC_WORKERS
    assert per_w % SC_WINDOW == 0

    @functools.partial(
        pl.kernel, out_type=jax.ShapeDtypeStruct((n_rows, d), x.dtype), mesh=_sc_mesh(),
        scratch_types=[pltpu.VMEM((SC_WINDOW,), jnp.int32), pltpu.VMEM((SC_WINDOW, d), x.dtype),
                       pltpu.SemaphoreType.DMA])
    def scatter(x_hbm, i0_hbm, i1_hbm, o_hbm, idx_v, rows_v, sem):
        wid = _sc_worker_id()

        @pl.loop(0, per_w // SC_WINDOW)
        def _(j):
            base = wid * per_w + j * SC_WINDOW
            pltpu.sync_copy(x_hbm.at[pl.ds(base, SC_WINDOW)], rows_v)
            for i_hbm in (i0_hbm, i1_hbm):
                pltpu.sync_copy(i_hbm.at[pl.ds(base, SC_WINDOW)], idx_v)
                pltpu.async_copy(rows_v, o_hbm.at[idx_v], sem).wait()

    return scatter(x, i0, i1)


def _sc_gather2(table, i0, i1):
    m = i0.shape[0]
    d = table.shape[1]
    per_w = m // SC_WORKERS
    assert per_w % SC_WINDOW == 0
    out = jax.ShapeDtypeStruct((m, d), table.dtype)

    @functools.partial(
        pl.kernel, out_type=(out, out), mesh=_sc_mesh(),
        scratch_types=[pltpu.VMEM((SC_WINDOW,), jnp.int32), pltpu.VMEM((SC_WINDOW, d), table.dtype),
                       pltpu.SemaphoreType.DMA])
    def gather(t_hbm, i0_hbm, i1_hbm, a_hbm, b_hbm, idx_v, rows_v, sem):
        wid = _sc_worker_id()

        @pl.loop(0, per_w // SC_WINDOW)
        def _(j):
            base = wid * per_w + j * SC_WINDOW
            for i_hbm, o_hbm in ((i0_hbm, a_hbm), (i1_hbm, b_hbm)):
                pltpu.sync_copy(i_hbm.at[pl.ds(base, SC_WINDOW)], idx_v)
                pltpu.async_copy(t_hbm.at[idx_v], rows_v, sem).wait()
                pltpu.sync_copy(rows_v, o_hbm.at[pl.ds(base, SC_WINDOW)])

    return gather(table, i0, i1)


def _experts_kernel(nf, te_ref, tv_ref, h_ref, w1_ref, w3_ref, w2_ref, o_ref):
    valid = tv_ref[pl.program_id(0)]

    @pl.when(valid > 0)
    def _():
        row = lax.broadcasted_iota(jnp.int32, h_ref.shape, 0)
        h = jnp.where(row < valid, h_ref[...], 0.0).astype(BF16)
        o_ref[...] = _swiglu_acc(h, w1_ref.at[0], w3_ref.at[0], w2_ref.at[0], nf)

    @pl.when(valid <= 0)
    def _():
        o_ref[...] = jnp.zeros_like(o_ref)


def _experts(h_sorted, tile_expert, tile_valid, w1, w3, w2, tile):
    n, d = h_sorted.shape
    f = w1.shape[-1]
    grid_spec = pltpu.PrefetchScalarGridSpec(
        num_scalar_prefetch=2,
        grid=(n // tile,),
        in_specs=[
            pl.BlockSpec((tile, d), lambda i, te, tv: (i, 0)),
            pl.BlockSpec((1, d, f), lambda i, te, tv: (te[i], 0, 0), pipeline_mode=pl.Buffered(1)),
            pl.BlockSpec((1, d, f), lambda i, te, tv: (te[i], 0, 0), pipeline_mode=pl.Buffered(1)),
            pl.BlockSpec((1, f, d), lambda i, te, tv: (te[i], 0, 0), pipeline_mode=pl.Buffered(1)),
        ],
        out_specs=pl.BlockSpec((tile, d), lambda i, te, tv: (i, 0)),
    )
    return pl.pallas_call(
        functools.partial(_experts_kernel, 2),
        grid_spec=grid_spec,
        out_shape=jax.ShapeDtypeStruct((n, d), F32),
        compiler_params=_cparams("arbitrary"),
        name="experts",
    )(tile_expert, tile_valid, h_sorted, w1, w3, w2)


def _combine_kernel(x1_ref, ya_ref, yb_ref, rt_ref, mod_ref, fw_ref, o_ref):
    m = mod_ref[0]
    rt = rt_ref[0]
    moe = rt[:, 2:3] * ya_ref[0] + rt[:, 3:4] * yb_ref[0]
    o_ref[0] = _rms(x1_ref[0] + m[5:6] * moe) * fw_ref[...]


def _moe(x1, h2, route, mod_l, mod_row, w1, w3, w2, final_w, tm, tile):
    bsz, s, d = x1.shape
    n = bsz * s
    ex = route[..., :2].astype(jnp.int32).reshape(2 * n)
    onehot = (ex[:, None] == jnp.arange(N_EXPERTS, dtype=jnp.int32)[None, :]).astype(jnp.int32)
    csum = jnp.cumsum(onehot, axis=0)
    cnt = csum[-1]
    rank = jnp.sum((csum - onehot) * onehot, axis=1)
    padded = ((cnt + tile - 1) // tile) * tile
    ends = jnp.cumsum(padded)
    pos = (ends - padded)[ex] + rank
    n_rows = 2 * n + N_EXPERTS * tile
    tile_start = jnp.arange(n_rows // tile, dtype=jnp.int32) * tile
    tile_expert = jnp.minimum(jnp.searchsorted(ends, tile_start, side="right"), N_EXPERTS - 1).astype(jnp.int32)
    group_end = (ends - padded + cnt)[tile_expert]
    tile_valid = jnp.clip(group_end - tile_start, 0, tile).astype(jnp.int32)

    pos2 = pos.reshape(n, 2)
    h_sorted = _sc_scatter2(h2.reshape(n, d), pos2[:, 0], pos2[:, 1], n_rows)
    y_sorted = _experts(h_sorted, tile_expert, tile_valid, w1, w3, w2, tile)
    ya, yb = _sc_gather2(y_sorted, pos2[:, 0], pos2[:, 1])
    ya = ya.reshape(bsz, s, d)
    yb = yb.reshape(bsz, s, d)
    tok = pl.BlockSpec((1, tm, d), lambda b, i: (b, i, 0))
    return pl.pallas_call(
        _combine_kernel,
        grid=(bsz, s // tm),
        in_specs=[tok, tok, tok,
                  pl.BlockSpec((1, tm, LANE), lambda b, i: (b, i, 0)),
                  pl.BlockSpec((1, N_MOD, d), lambda b, i: (mod_row(b), 0, 0)),
                  _resident((1, d))],
        out_specs=tok,
        out_shape=jax.ShapeDtypeStruct((bsz, s, d), F32),
        compiler_params=_cparams("parallel", "arbitrary"),
        name="moe_combine",
    )(x1, ya, yb, route, mod_l, final_w[None])


def _pick_tile(s, pref):
    t = min(s, pref)
    while s % t:
        t //= 2
    return t


def kernel(x, c, ctx, c_ctx, mod_w, mod_b, norm1_w, norm2_w, w_in, ssd_conv_w, ssd_conv_b, ssd_dt_bias,
           ssd_a_log, ssd_d, ssd_norm_w, lru_conv_w, lru_conv_b, lru_rw, lru_rb, lru_iw, lru_ib, lru_lambda,
           lru_norm_w, w_out, ffn_w1, ffn_w3, ffn_w2, router_w, moe_w1, moe_w3, moe_w2, final_norm_w):
    depth = w_in.shape[0]
    bsz, seq, d = x.shape
    lc = ctx.shape[1]
    rows = seq // GRID_W
    heads = ssd_dt_bias.shape[-1]
    d_ssd = heads * SSD_HEADDIM
    d_xbc = ssd_conv_w.shape[-1]
    d_lru = lru_conv_w.shape[-1]
    splits = (d_ssd, d_xbc, d_lru, d_lru)
    off_dt = d_ssd + d_xbc
    off_lx = off_dt + 2 * heads
    perm = jnp.concatenate([jnp.arange(0, off_dt), jnp.arange(off_lx, off_lx + 2 * d_lru),
                            jnp.arange(off_dt, off_lx)])
    ctx_cols = 8
    ctx_rows = lc // ctx_cols
    lru_cw = 256

    mod = _modulation(c, c_ctx, mod_w, mod_b)
    lat_row = lambda b: b
    ctx_row = lambda b: bsz
    tm_lat = _pick_tile(seq, 512)
    tm_ctx = _pick_tile(lc, 512)
    tb_lat = _pick_tile(seq, 512)
    tb_ctx = _pick_tile(lc, 512)
    gw = d_ssd // SSD_GROUPS

    xc = ctx
    for l in range(depth):
        last = l == depth - 1
        mod_l = mod[l]
        w_perm = jnp.pad(w_in[l][:, perm], ((0, 0), (0, LANE - 2 * heads))).astype(BF16)
        ssd_p = dict(conv_w=ssd_conv_w[l], conv_b=ssd_conv_b[l], dt_bias=ssd_dt_bias[l], a_log=ssd_a_log[l],
                     d_skip=ssd_d[l], norm_w=ssd_norm_w[l])
        lru_p = _lru_params(lru_conv_w[l], lru_conv_b[l], lru_rw[l], lru_rb[l], lru_iw[l], lru_ib[l],
                            lru_lambda[l], lru_cw)
        wo = w_out[l].astype(BF16)

        zc, xbcc, lxc, lgc, dtc = _inproj(xc, mod_l, ctx_row, norm1_w[l], w_perm, splits, 2 * heads, tm_ctx)
        zl, xbcl, lxl, lgl, dtl = _inproj(x, mod_l, lat_row, norm1_w[l], w_perm, splits, 2 * heads, tm_lat)

        zero_s = jnp.zeros((bsz, SSD_GROUPS, SSD_STATE, gw), F32)
        yc_ssd, s_f, s_b = _ssd_bidir(zc, xbcc, dtc, ssd_p, zero_s, zero_s, tb_ctx)
        yl_ssd, _, _ = _ssd_bidir(zl, xbcl, dtl, ssd_p, s_f, s_b, tb_lat)

        to_grid = lambda v: v.reshape(bsz, ctx_cols, ctx_rows, d_lru).transpose(0, 2, 1, 3)
        gc, h_fin = _lru(to_grid(lxc), to_grid(lgc), lru_p, jnp.zeros((bsz, 2, d_lru), F32), ctx_rows, True)
        gl, _ = _lru(lxl.reshape(bsz, rows, GRID_W, d_lru), lgl.reshape(bsz, rows, GRID_W, d_lru), lru_p,
                     h_fin, 8, False)
        gl = gl.reshape(bsz, seq, d_lru)

        if l % 2 == 0:
            w1, w3, w2 = (ffn_w1[l // 2].astype(BF16), ffn_w3[l // 2].astype(BF16), ffn_w2[l // 2].astype(BF16))
            x1, h2 = _outproj(yl_ssd, gl, x, mod_l, lat_row, lru_norm_w[l], wo, norm2_w[l], tm_lat)
            x = _ffn(x1, h2, mod_l, lat_row, w1, w3, w2, tm_lat, final_norm_w if last else None)
            if not last:
                gc = gc.transpose(0, 2, 1, 3).reshape(bsz, lc, d_lru)
                xc1, hc2 = _outproj(yc_ssd, gc, xc, mod_l, ctx_row, lru_norm_w[l], wo, norm2_w[l], tm_ctx)
                xc = _ffn(xc1, hc2, mod_l, ctx_row, w1, w3, w2, tm_ctx)
        else:
            w1, w3, w2 = (moe_w1[l // 2].astype(BF16), moe_w3[l // 2].astype(BF16), moe_w2[l // 2].astype(BF16))
            assert last, "a routed layer that is not the last layer is not implemented"
            x1, h2, route = _outproj(yl_ssd, gl, x, mod_l, lat_row, lru_norm_w[l], wo, norm2_w[l], tm_lat,
                                     router_w=router_w[l // 2])
            x = _moe(x1, h2, route, mod_l, lat_row, w1, w3, w2, final_norm_w, tm_lat, 512)
    return x
```

```python
import functools

import jax
import jax.numpy as jnp
from jax import lax
from jax.experimental import pallas as pl
from jax.experimental.pallas import tpu as pltpu
from jax.experimental.pallas import tpu_sc as plsc

F32 = jnp.float32
BF16 = jnp.bfloat16
HI = lax.Precision.HIGHEST

EPS = 1e-6
N_MOD = 6
GRID_W = 64
SSD_HEADDIM = 64
SSD_GROUPS = 2
SSD_STATE = 128
SSD_CHUNK = 128
CONV_K = 4
LRU_BW = 64
RGLRU_C = 8.0
N_EXPERTS = 8
LOG2E = 1.4426950408889634
F32_TINY = float(jnp.finfo(jnp.float32).tiny)
HALO = 16
LANE = 128
MOD_ROWS = 24
VMEM_LIMIT = 56 * 1024 * 1024
SC_CORES = 2
SC_WORKERS = 32
SC_WINDOW = 64


def _cparams(*sem):
    return pltpu.CompilerParams(dimension_semantics=sem, vmem_limit_bytes=VMEM_LIMIT)


def _resident(shape):
    nd = len(shape)
    return pl.BlockSpec(shape, lambda *_: (0,) * nd, pipeline_mode=pl.Buffered(1))


def _rms(x):
    return x * lax.rsqrt(jnp.mean(x * x, axis=-1, keepdims=True) + EPS)


def _pack_pairs(v):
    k = v.shape[1] // 2
    lo = lax.bitcast_convert_type(v[:, :k].astype(BF16).astype(F32), jnp.uint32)
    hi = lax.bitcast_convert_type(v[:, k:].astype(BF16).astype(F32), jnp.uint32)
    return hi | (lo >> 16)


def _unpack_pairs(w):
    lo = lax.bitcast_convert_type(w << 16, F32)
    hi = lax.bitcast_convert_type(w & jnp.uint32(0xFFFF0000), F32)
    return lo, hi


def _mod_kernel(s_ref, w_ref, b_ref, o_ref):
    s = s_ref[...]
    s = s * jax.nn.sigmoid(s)
    o_ref[0] = jnp.dot(s, w_ref[0], precision=HI, preferred_element_type=F32) + b_ref[0]


def _modulation(c, c_ctx, mod_w, mod_b):
    depth, d, n = mod_w.shape
    bsz = c.shape[0]
    assert bsz + 1 <= MOD_ROWS
    s = jnp.concatenate([c, c_ctx[None], jnp.zeros((MOD_ROWS - bsz - 1, d), F32)], axis=0)
    tn = n // 4
    out = pl.pallas_call(
        _mod_kernel,
        grid=(depth, n // tn),
        in_specs=[
            pl.BlockSpec((MOD_ROWS, d), lambda l, j: (0, 0)),
            pl.BlockSpec((1, d, tn), lambda l, j: (l, 0, j)),
            pl.BlockSpec((1, 1, tn), lambda l, j: (l, 0, j)),
        ],
        out_specs=pl.BlockSpec((1, MOD_ROWS, tn), lambda l, j: (l, 0, j)),
        out_shape=jax.ShapeDtypeStruct((depth, MOD_ROWS, n), F32),
        compiler_params=_cparams("arbitrary", "arbitrary"),
        name="modulation",
    )(s, mod_w, mod_b[:, None, :])
    return out.reshape(depth, MOD_ROWS, N_MOD, d)


def _inproj_kernel(splits, dt_w, nt, x_ref, xprev_ref, xnext_ref, mod_ref, nw_ref, w_ref, cw_ref, cb_ref,
                   z_ref, xs_ref, lx_ref, lg_ref, dt_ref, xe_ref):
    i = pl.program_id(1)
    m = mod_ref[0]
    tm = x_ref.shape[1]

    def norm_mod(v):
        return (_rms(v) * nw_ref[...] * (1.0 + m[1:2]) + m[0:1]).astype(BF16)

    h = norm_mod(x_ref[0])

    def proj(a, b):
        return jnp.dot(h, w_ref[:, a:b], preferred_element_type=F32)

    o = [0]
    for s in splits:
        o.append(o[-1] + s)
    z_ref[0] = proj(o[0], o[1]).astype(BF16)
    lx_ref[0] = proj(o[2], o[3]).astype(BF16)
    lg_ref[0] = proj(o[3], o[4]).astype(BF16)
    dt_ref[0] = proj(o[4], o[4] + LANE)[:, :dt_w]

    h_ext = jnp.concatenate([norm_mod(xprev_ref[0, 0]), h, norm_mod(xnext_ref[0, 0])], axis=0)
    xe = jnp.dot(h_ext, w_ref[:, o[1]:o[2]], preferred_element_type=F32)
    row = lax.broadcasted_iota(jnp.int32, (tm + 2 * HALO, 1), 0)
    outside = jnp.logical_or(jnp.logical_and(row < HALO, i == 0),
                             jnp.logical_and(row >= tm + HALO, i == nt - 1))
    xe_ref[...] = jnp.where(outside, 0.0, xe)
    acc = cb_ref[...] + cw_ref[0:1] * xe_ref[HALO - 2:HALO - 2 + tm]
    for k in range(1, CONV_K):
        acc = acc + cw_ref[k:k + 1] * xe_ref[HALO - 2 + k:HALO - 2 + k + tm]
    xs_ref[0] = (acc * jax.nn.sigmoid(acc)).astype(BF16)


def _inproj(x, mod_l, mod_row, norm_w, w_perm, conv_w, conv_b, splits, dt_w, tm):
    bsz, s, d = x.shape
    n_all = w_perm.shape[1]
    nt = s // tm
    nh = tm // HALO
    xh = x.reshape(bsz, s // HALO, HALO, d)
    outs = [jax.ShapeDtypeStruct((bsz, s, w), BF16) for w in splits] + [jax.ShapeDtypeStruct((bsz, s, dt_w), F32)]
    return pl.pallas_call(
        functools.partial(_inproj_kernel, splits, dt_w, nt),
        grid=(bsz, nt),
        in_specs=[
            pl.BlockSpec((1, tm, d), lambda b, i: (b, i, 0)),
            pl.BlockSpec((1, 1, HALO, d), lambda b, i: (b, jnp.maximum(i * nh - 1, 0), 0, 0)),
            pl.BlockSpec((1, 1, HALO, d), lambda b, i: (b, jnp.minimum((i + 1) * nh, s // HALO - 1), 0, 0)),
            pl.BlockSpec((1, N_MOD, d), lambda b, i: (mod_row(b), 0, 0)),
            _resident((1, d)),
            _resident((d, n_all)),
            _resident((CONV_K, splits[1])),
            _resident((1, splits[1])),
        ],
        out_specs=[pl.BlockSpec((1, tm, w), lambda b, i: (b, i, 0)) for w in splits]
        + [pl.BlockSpec((1, tm, dt_w), lambda b, i: (b, i, 0))],
        out_shape=outs,
        scratch_shapes=[pltpu.VMEM((tm + 2 * HALO, splits[1]), F32)],
        compiler_params=_cparams("parallel", "arbitrary"),
        name="inproj",
    )(x, xh, xh, mod_l, norm_w[None], w_perm, conv_w, conv_b[None])


def _softplus(x):
    return jnp.maximum(x, 0.0) + jnp.log1p(jnp.exp(-jnp.abs(x)))


def _ssd_kernel(rev, nblk, tb, heads, *refs):
    if rev:
        (xs_ref, dt_ref, dtb_ref, alog_ref, s0_ref,
         yf_ref, z_ref, dsk_ref, nw_ref, y_ref, sfin_ref, st_ref, yb_ref) = refs
    else:
        (xs_ref, dt_ref, dtb_ref, alog_ref, s0_ref, y_ref, sfin_ref, st_ref) = refs
        yb_ref = y_ref.at[0]
    i = pl.program_id(1)
    d_ssd = heads * SSD_HEADDIM
    gn = SSD_GROUPS * SSD_STATE
    gw = d_ssd // SSD_GROUPS
    ck = SSD_CHUNK

    @pl.when(i == 0)
    def _():
        st_ref[...] = s0_ref[0]

    d0 = heads if rev else 0
    dt = _softplus(dt_ref[0] + dtb_ref[...])[:, d0:d0 + heads]
    a_neg = -jnp.exp(alog_ref[...])[:, d0:d0 + heads]
    da = dt * a_neg
    ldt = jnp.where(dt > 0.0, jnp.log(dt), -1e30)

    ri = lax.broadcasted_iota(jnp.int32, (ck, ck), 0)
    ci = lax.broadcasted_iota(jnp.int32, (ck, ck), 1)
    tri = (ci >= ri) if rev else (ci <= ri)
    tri_f = tri.astype(F32)
    tri_t = ((ri >= ci) if rev else (ri <= ci)).astype(F32)
    eye = (ri == ci).astype(F32)
    rhs_rt = jnp.concatenate([tri_t, eye], axis=0)
    eh = lax.broadcasted_iota(jnp.int32, (heads, d_ssd), 0)
    ec = lax.broadcasted_iota(jnp.int32, (heads, d_ssd), 1)
    expand = jnp.where(ec // SSD_HEADDIM == eh, 1.0, 0.0).astype(BF16)
    expand2 = jnp.concatenate([expand, expand], axis=0)
    lane = lax.broadcasted_iota(jnp.int32, (ck, LANE), 1)
    end = 0 if rev else ck - 1

    def widen(q):
        hi = q.astype(BF16)
        lo = (q - hi.astype(F32)).astype(BF16)
        return jnp.dot(jnp.concatenate([hi, lo], axis=1), expand2, preferred_element_type=F32)

    order = range(tb // ck - 1, -1, -1) if rev else range(tb // ck)
    for c in order:
        r0 = c * ck
        x_bf = xs_ref[0, r0:r0 + ck, :d_ssd]
        b_bf = xs_ref[0, r0:r0 + ck, d_ssd:d_ssd + gn]
        c_bf = xs_ref[0, r0:r0 + ck, d_ssd + gn:]
        dac = da[r0:r0 + ck]
        cs = jnp.dot(tri_f, dac, precision=HI, preferred_element_type=F32)
        r_t = lax.dot_general(jnp.concatenate([dac, -ldt[r0:r0 + ck]], axis=0), rhs_rt,
                              (((0,), (0,)), ((), ())), precision=HI,
                              preferred_element_type=F32)
        ecs = jnp.exp(cs)
        tot = cs[end:end + 1]
        wgt = jnp.exp(tot - cs) * dt[r0:r0 + ck]
        ecs_w = widen(ecs)
        xw = (x_bf.astype(F32) * widen(wgt)).astype(BF16)
        cs2 = cs * LOG2E
        r_t2 = r_t * LOG2E
        for g in range(SSD_GROUPS):
            bg = b_bf[:, g * SSD_STATE:(g + 1) * SSD_STATE]
            cg = c_bf[:, g * SSD_STATE:(g + 1) * SSD_STATE]
            cb = lax.dot_general(cg, bg, (((1,), (1,)), ((), ())), preferred_element_type=F32)
            s_in = st_ref[g]
            y_off = jnp.dot(cg, s_in.astype(BF16), preferred_element_type=F32) * ecs_w[:, g * gw:(g + 1) * gw]
            new = lax.dot_general(bg, xw[:, g * gw:(g + 1) * gw], (((0,), (0,)), ((), ())),
                                  preferred_element_type=F32)
            st_ref[g] = ecs_w[end:end + 1, g * gw:(g + 1) * gw] * s_in + new
            for p in range(gw // LANE):
                ha = (g * gw + p * LANE) // SSD_HEADDIM
                lm = []
                for h in (ha, ha + 1):
                    seg2 = cs2[:, h:h + 1] - r_t2[h:h + 1, :]
                    lm.append(jnp.where(tri, jnp.exp2(seg2), 0.0) * cb)
                lhs = jnp.concatenate(lm, axis=1).astype(BF16)
                xpair = x_bf[:, g * gw + p * LANE:g * gw + (p + 1) * LANE]
                zero = jnp.zeros_like(xpair)
                rhs = jnp.concatenate([jnp.where(lane < SSD_HEADDIM, xpair, zero),
                                       jnp.where(lane >= SSD_HEADDIM, xpair, zero)], axis=0)
                col = g * gw + p * LANE
                yb_ref[r0:r0 + ck, col:col + LANE] = (
                    jnp.dot(lhs, rhs, preferred_element_type=F32) + y_off[:, p * LANE:(p + 1) * LANE])

    if rev:
        z = z_ref[0].astype(F32)
        x_all = xs_ref[0, :, :d_ssd].astype(F32)
        tot_y = (yb_ref[...] + yf_ref[0] + dsk_ref[...] * x_all) * (z * jax.nn.sigmoid(z))
        y_ref[0] = (_rms(tot_y) * nw_ref[...]).astype(BF16)

    @pl.when(i == nblk - 1)
    def _():
        sfin_ref[0] = st_ref[...]


def _ssd_dir(rev, xs, dt_raw, dt_bias, a_log, s0, tb, extra=None):
    bsz, s, dxbc = xs.shape
    heads = dt_raw.shape[-1] // 2
    d_ssd = heads * SSD_HEADDIM
    gw = d_ssd // SSD_GROUPS
    nblk = s // tb
    pos = (lambda i: nblk - 1 - i) if rev else (lambda i: i)
    in_specs = [
        pl.BlockSpec((1, tb, dxbc), lambda b, i: (b, pos(i), 0)),
        pl.BlockSpec((1, tb, 2 * heads), lambda b, i: (b, pos(i), 0)),
        _resident((1, 2 * heads)),
        _resident((1, 2 * heads)),
        pl.BlockSpec((1, SSD_GROUPS, SSD_STATE, gw), lambda b, i: (b, 0, 0, 0)),
    ]
    args = [xs, dt_raw, dt_bias.reshape(1, -1), a_log.reshape(1, -1), s0]
    st_shape = jax.ShapeDtypeStruct((bsz, SSD_GROUPS, SSD_STATE, gw), F32)
    st_spec = pl.BlockSpec((1, SSD_GROUPS, SSD_STATE, gw), lambda b, i: (b, 0, 0, 0))
    scratch = [pltpu.VMEM((SSD_GROUPS, SSD_STATE, gw), F32)]
    if rev:
        yf, z, d_skip, norm_w = extra
        in_specs += [
            pl.BlockSpec((1, tb, d_ssd), lambda b, i: (b, pos(i), 0)),
            pl.BlockSpec((1, tb, d_ssd), lambda b, i: (b, pos(i), 0)),
            _resident((1, d_ssd)),
            _resident((1, d_ssd)),
        ]
        args += [yf, z, jnp.repeat(d_skip, SSD_HEADDIM)[None], norm_w[None]]
        y_dtype = BF16
        scratch.append(pltpu.VMEM((tb, d_ssd), F32))
    else:
        y_dtype = F32
    return pl.pallas_call(
        functools.partial(_ssd_kernel, rev, nblk, tb, heads),
        grid=(bsz, nblk),
        in_specs=in_specs,
        out_specs=[pl.BlockSpec((1, tb, d_ssd), lambda b, i: (b, pos(i), 0)), st_spec],
        out_shape=[jax.ShapeDtypeStruct((bsz, s, d_ssd), y_dtype), st_shape],
        scratch_shapes=scratch,
        compiler_params=_cparams("parallel", "arbitrary"),
        name="ssd_bwd" if rev else "ssd_fwd",
    )(*args)


def _ssd_bidir(z, xs, dt_raw, p, s0_f, s0_b, tb):
    yf, s_f = _ssd_dir(False, xs, dt_raw, p["dt_bias"], p["a_log"], s0_f, tb)
    y, s_b = _ssd_dir(True, xs, dt_raw, p["dt_bias"], p["a_log"], s0_b, tb,
                      extra=(yf, z, p["d_skip"], p["norm_w"]))
    return y, s_f, s_b


def _shift_rows(x, s, fill, up):
    c = x.shape[0]
    rolled = pltpu.roll(x, (c - s) if up else s, 0)
    row = lax.broadcasted_iota(jnp.int32, x.shape, 0)
    keep = (row < c - s) if up else (row >= s)
    return jnp.where(keep, rolled, fill)


def _lru_kernel(nr, nc, rb, continuous, lx_ref, lg_ref, cw_ref, cb_ref, wg_ref, bg_ref, spl_ref, h0_ref,
                g_ref, hfin_ref, xp_ref, a_ref, b_ref):
    cwid = lx_ref.shape[-1]
    xin = lx_ref[0].astype(F32)
    xp_ref[2:nr + 2] = xin
    if continuous:
        for k in range(2):
            xp_ref[k] = _shift_rows(xin[nr - 2 + k], 1, 0.0, up=False)
        xp_ref[nr + 2] = _shift_rows(xin[0], 1, 0.0, up=True)
    else:
        xp_ref[0:2] = jnp.zeros((2, nc, cwid), F32)
        xp_ref[nr + 2:nr + 3] = jnp.zeros((1, nc, cwid), F32)

    def gates(t, _):
        r0 = pl.multiple_of(t * rb, rb)
        xc = cb_ref[...] + cw_ref[0:1] * xp_ref[pl.ds(r0, rb)]
        for k in range(1, CONV_K):
            xc = xc + cw_ref[k:k + 1] * xp_ref[pl.ds(r0 + k, rb)]
        xc2 = xc.reshape(rb * nc, cwid)
        gt = jnp.dot(xc2.astype(BF16), wg_ref[0], preferred_element_type=F32) + bg_ref[0]
        x_half = 0.5 * xc2
        for d in range(2):
            t_r = jnp.tanh(gt[:, (2 * d) * cwid:(2 * d + 1) * cwid])
            t_i = jnp.tanh(gt[:, (2 * d + 1) * cwid:(2 * d + 2) * cwid])
            c_half = spl_ref[0, d:d + 1]
            nla = c_half * t_r + c_half
            a = jnp.exp2(nla * (-LOG2E))
            one_m_a2 = jnp.tanh(nla) * (a * a + 1.0)
            root = one_m_a2 * lax.rsqrt(jnp.maximum(one_m_a2, F32_TINY))
            a_ref[d, pl.ds(r0, rb)] = a.reshape(rb, nc, cwid)
            b_ref[d, pl.ds(r0, rb)] = (root * (t_i + 1.0) * x_half).reshape(rb, nc, cwid)
        return 0

    lax.fori_loop(0, nr // rb, gates, 0)

    hin = []
    for d in range(2):
        def step(t, carry):
            h, p = carry
            r = (nr - 1 - t) if d else t
            a = a_ref[d, r]
            h = a * h + b_ref[d, r]
            p = a * p
            b_ref[d, r] = h
            a_ref[d, r] = p
            return h, p

        h_l, p_l = lax.fori_loop(0, nr, step, (jnp.zeros((nc, cwid), F32), jnp.ones((nc, cwid), F32)))
        sa, sb = p_l, h_l
        s = 1
        while s < nc:
            sb = sa * _shift_rows(sb, s, 0.0, up=bool(d)) + sb
            sa = sa * _shift_rows(sa, s, 1.0, up=bool(d))
            s *= 2
        h0 = h0_ref[0, d:d + 1]
        after = sa * h0 + sb
        hin.append(_shift_rows(after, 1, h0, up=bool(d)) if nc > 1 else jnp.broadcast_to(h0, (nc, cwid)))
        last = 0 if d else nc - 1
        hfin_ref[0, d:d + 1] = after[last:last + 1]

    def finish(t, _):
        r0 = pl.multiple_of(t * rb, rb)
        h = (b_ref[0, pl.ds(r0, rb)] + a_ref[0, pl.ds(r0, rb)] * hin[0]
             + b_ref[1, pl.ds(r0, rb)] + a_ref[1, pl.ds(r0, rb)] * hin[1])
        g_ref[0, pl.ds(r0, rb)] = (h * jax.nn.gelu(lg_ref[0, pl.ds(r0, rb)].astype(F32))).astype(BF16)
        return 0

    lax.fori_loop(0, nr // rb, finish, 0)


def _lru(lx, lg, p, h0, rb, continuous):
    bsz, nr, nc, d = lx.shape
    cwid = p["wg"].shape[1]
    ng = d // cwid
    return pl.pallas_call(
        functools.partial(_lru_kernel, nr, nc, rb, continuous),
        grid=(bsz, ng),
        in_specs=[
            pl.BlockSpec((1, nr, nc, cwid), lambda b, j: (b, 0, 0, j)),
            pl.BlockSpec((1, nr, nc, cwid), lambda b, j: (b, 0, 0, j)),
            pl.BlockSpec((CONV_K, cwid), lambda b, j: (0, j)),
            pl.BlockSpec((1, cwid), lambda b, j: (0, j)),
            pl.BlockSpec((1, cwid, 4 * cwid), lambda b, j: (j, 0, 0)),
            pl.BlockSpec((1, 1, 4 * cwid), lambda b, j: (j, 0, 0)),
            pl.BlockSpec((1, 2, cwid), lambda b, j: (j, 0, 0)),
            pl.BlockSpec((1, 2, cwid), lambda b, j: (b, 0, j)),
        ],
        out_specs=[
            pl.BlockSpec((1, nr, nc, cwid), lambda b, j: (b, 0, 0, j)),
            pl.BlockSpec((1, 2, cwid), lambda b, j: (b, 0, j)),
        ],
        out_shape=[jax.ShapeDtypeStruct((bsz, nr, nc, d), BF16), jax.ShapeDtypeStruct((bsz, 2, d), F32)],
        scratch_shapes=[
            pltpu.VMEM((nr + 3, nc, cwid), F32),
            pltpu.VMEM((2, nr, nc, cwid), F32),
            pltpu.VMEM((2, nr, nc, cwid), F32),
        ],
        compiler_params=_cparams("parallel", "arbitrary"),
        name="rglru",
    )(lx, lg, p["conv_w"], p["conv_b"][None], p["wg"], p["bg"], p["spl"], h0)


def _lru_params(conv_w, conv_b, rw, rb_, iw, ib, lam, cwid):
    d = conv_w.shape[-1]
    ng = d // cwid
    nb = cwid // LRU_BW

    def block_diag(w):
        w = w.reshape(ng, nb, LRU_BW, LRU_BW)
        eye = jnp.eye(nb, dtype=w.dtype)
        return jnp.einsum("gakj,ab->gakbj", w, eye).reshape(ng, cwid, cwid)

    wg = jnp.concatenate([block_diag(rw[0]), block_diag(iw[0]), block_diag(rw[1]), block_diag(iw[1])], axis=-1)
    bg = jnp.concatenate([rb_[0].reshape(ng, cwid), ib[0].reshape(ng, cwid),
                          rb_[1].reshape(ng, cwid), ib[1].reshape(ng, cwid)], axis=-1)[:, None, :]
    spl = (0.5 * RGLRU_C) * jax.nn.softplus(-lam.astype(F32)).reshape(2, ng, cwid).transpose(1, 0, 2)
    return dict(conv_w=conv_w, conv_b=conv_b, wg=(0.5 * wg).astype(BF16), bg=0.5 * bg, spl=spl)


def _outproj_kernel(route, y_ref, g_ref, x_ref, mod_ref, lnw_ref, wo_ref, n2w_ref, *rest):
    if route:
        rw_ref, x1_ref, h2_ref, rt_ref = rest
    else:
        x1_ref, h2_ref = rest
    d = x_ref.shape[-1]
    m = mod_ref[0]
    gl = (_rms(g_ref[0].astype(F32)) * lnw_ref[...]).astype(BF16)
    dy = y_ref.shape[-1]
    proj = (jnp.dot(y_ref[0], wo_ref[0:dy], preferred_element_type=F32)
            + jnp.dot(gl, wo_ref[dy:], preferred_element_type=F32))
    x1 = x_ref[0] + m[2:3] * proj
    x1_ref[0] = x1
    h2 = _rms(x1) * n2w_ref[...] * (1.0 + m[4:5]) + m[3:4]
    if not route:
        h2_ref[0] = h2.astype(BF16)
        return
    h2_ref[0] = _pack_pairs(h2)
    h_hi = h2.astype(BF16)
    h_lo = (h2 - h_hi.astype(F32)).astype(BF16)
    r_hi = jnp.dot(h_hi, rw_ref[...], preferred_element_type=F32)
    logits = (r_hi[:, :LANE] + r_hi[:, LANE:]
              + jnp.dot(h_lo, rw_ref[:, :LANE], preferred_element_type=F32))
    lane = lax.broadcasted_iota(jnp.int32, logits.shape, 1).astype(F32)
    neg = -jnp.inf
    logits = jnp.where(lane < N_EXPERTS, logits, neg)
    m1 = jnp.max(logits, axis=-1, keepdims=True)
    i1 = jnp.min(jnp.where(logits == m1, lane, float(LANE)), axis=-1, keepdims=True)
    rest_l = jnp.where(lane == i1, neg, logits)
    m2 = jnp.max(rest_l, axis=-1, keepdims=True)
    i2 = jnp.min(jnp.where(rest_l == m2, lane, float(LANE)), axis=-1, keepdims=True)
    e = jnp.exp(m2 - m1)
    p1 = 1.0 / (1.0 + e)
    p2 = e / (1.0 + e)
    rt_ref[0] = jnp.where(lane == 0.0, i1, jnp.where(lane == 1.0, i2,
                          jnp.where(lane == 2.0, p1, jnp.where(lane == 3.0, p2, 0.0))))


def _outproj(y, g, x, mod_l, mod_row, lru_norm_w, w_out, norm2_w, tm, router_w=None):
    bsz, s, d = x.shape
    route = router_w is not None
    in_specs = [
        pl.BlockSpec((1, tm, y.shape[-1]), lambda b, i: (b, i, 0)),
        pl.BlockSpec((1, tm, g.shape[-1]), lambda b, i: (b, i, 0)),
        pl.BlockSpec((1, tm, d), lambda b, i: (b, i, 0)),
        pl.BlockSpec((1, N_MOD, d), lambda b, i: (mod_row(b), 0, 0)),
        _resident((1, g.shape[-1])),
        _resident(w_out.shape),
        _resident((1, d)),
    ]
    args = [y, g, x, mod_l, lru_norm_w[None], w_out, norm2_w[None]]
    dh = d // 2 if route else d
    out_specs = [pl.BlockSpec((1, tm, d), lambda b, i: (b, i, 0)), pl.BlockSpec((1, tm, dh), lambda b, i: (b, i, 0))]
    out_shape = [jax.ShapeDtypeStruct((bsz, s, d), F32),
                 jax.ShapeDtypeStruct((bsz, s, dh), jnp.uint32 if route else BF16)]
    if route:
        rw = jnp.zeros((d, LANE), F32).at[:, :N_EXPERTS].set(router_w)
        rw_hi = rw.astype(BF16)
        in_specs.append(_resident((d, 2 * LANE)))
        args.append(jnp.concatenate([rw_hi, (rw - rw_hi.astype(F32)).astype(BF16)], axis=1))
        out_specs.append(pl.BlockSpec((1, tm, LANE), lambda b, i: (b, i, 0)))
        out_shape.append(jax.ShapeDtypeStruct((bsz, s, LANE), F32))
    return pl.pallas_call(
        functools.partial(_outproj_kernel, route),
        grid=(bsz, s // tm),
        in_specs=in_specs,
        out_specs=out_specs,
        out_shape=out_shape,
        compiler_params=_cparams("parallel", "arbitrary"),
        name="outproj_route" if route else "outproj",
    )(*args)


def _swiglu_acc(h, w1_ref, w3_ref, w2_ref, nf):
    f = w1_ref.shape[-1]
    fc = f // nf
    acc = None
    for j in range(nf):
        a = jnp.dot(h, w1_ref[:, j * fc:(j + 1) * fc], preferred_element_type=F32)
        b = jnp.dot(h, w3_ref[:, j * fc:(j + 1) * fc], preferred_element_type=F32)
        pj = (a * jax.nn.sigmoid(a) * b).astype(BF16)
        o = jnp.dot(pj, w2_ref[j * fc:(j + 1) * fc, :], preferred_element_type=F32)
        acc = o if acc is None else acc + o
    return acc


def _ffn_kernel(final, nf, x1_ref, h2_ref, mod_ref, w1_ref, w3_ref, w2_ref, *rest):
    if final:
        fw_ref, o_ref = rest
    else:
        (o_ref,) = rest
    m = mod_ref[0]
    out = x1_ref[0] + m[5:6] * _swiglu_acc(h2_ref[0], w1_ref, w3_ref, w2_ref, nf)
    if final:
        out = _rms(out) * fw_ref[...]
    o_ref[0] = out


def _ffn(x1, h2, mod_l, mod_row, w1, w3, w2, tm, final_w=None):
    bsz, s, d = x1.shape
    final = final_w is not None
    in_specs = [
        pl.BlockSpec((1, tm, d), lambda b, i: (b, i, 0)),
        pl.BlockSpec((1, tm, d), lambda b, i: (b, i, 0)),
        pl.BlockSpec((1, N_MOD, d), lambda b, i: (mod_row(b), 0, 0)),
        _resident(w1.shape), _resident(w3.shape), _resident(w2.shape),
    ]
    args = [x1, h2, mod_l, w1, w3, w2]
    if final:
        in_specs.append(_resident((1, d)))
        args.append(final_w[None])
    return pl.pallas_call(
        functools.partial(_ffn_kernel, final, 2),
        grid=(bsz, s // tm),
        in_specs=in_specs,
        out_specs=pl.BlockSpec((1, tm, d), lambda b, i: (b, i, 0)),
        out_shape=jax.ShapeDtypeStruct((bsz, s, d), F32),
        compiler_params=_cparams("parallel", "arbitrary"),
        name="ffn_dense",
    )(*args)


def _sc_worker_id():
    return lax.axis_index("subcore") * SC_CORES + lax.axis_index("core")


def _sc_mesh():
    return plsc.VectorSubcoreMesh(core_axis_name="core", subcore_axis_name="subcore")


def _sc_scatter2(x, i0, i1, n_rows):
    m, d = x.shape
    per_w = m // SC_WORKERS
    assert per_w % SC_WINDOW == 0

    @functools.partial(
        pl.kernel, out_type=jax.ShapeDtypeStruct((n_rows, d), x.dtype), mesh=_sc_mesh(),
        scratch_types=[pltpu.VMEM((SC_WINDOW,), jnp.int32), pltpu.VMEM((SC_WINDOW, d), x.dtype),
                       pltpu.SemaphoreType.DMA])
    def scatter(x_hbm, i0_hbm, i1_hbm, o_hbm, idx_v, rows_v, sem):
        wid = _sc_worker_id()

        @pl.loop(0, per_w // SC_WINDOW)
        def _(j):
            base = wid * per_w + j * SC_WINDOW
            pltpu.sync_copy(x_hbm.at[pl.ds(base, SC_WINDOW)], rows_v)
            for i_hbm in (i0_hbm, i1_hbm):
                pltpu.sync_copy(i_hbm.at[pl.ds(base, SC_WINDOW)], idx_v)
                pltpu.async_copy(rows_v, o_hbm.at[idx_v], sem).wait()

    return scatter(x, i0, i1)


def _sc_gather2(table, i0, i1):
    m = i0.shape[0]
    d = table.shape[1]
    per_w = m // SC_WORKERS
    assert per_w % SC_WINDOW == 0
    out = jax.ShapeDtypeStruct((m, d), table.dtype)

    @functools.partial(
        pl.kernel, out_type=(out, out), mesh=_sc_mesh(),
        scratch_types=[pltpu.VMEM((SC_WINDOW,), jnp.int32), pltpu.VMEM((SC_WINDOW, d), table.dtype),
                       pltpu.SemaphoreType.DMA])
    def gather(t_hbm, i0_hbm, i1_hbm, a_hbm, b_hbm, idx_v, rows_v, sem):
        wid = _sc_worker_id()

        @pl.loop(0, per_w // SC_WINDOW)
        def _(j):
            base = wid * per_w + j * SC_WINDOW
            for i_hbm, o_hbm in ((i0_hbm, a_hbm), (i1_hbm, b_hbm)):
                pltpu.sync_copy(i_hbm.at[pl.ds(base, SC_WINDOW)], idx_v)
                pltpu.async_copy(t_hbm.at[idx_v], rows_v, sem).wait()
                pltpu.sync_copy(rows_v, o_hbm.at[pl.ds(base, SC_WINDOW)])

    return gather(table, i0, i1)


def _experts_kernel(nf, te_ref, tv_ref, h_ref, w1_ref, w3_ref, w2_ref, o_ref):
    valid = tv_ref[pl.program_id(0)]

    @pl.when(valid > 0)
    def _():
        row = lax.broadcasted_iota(jnp.int32, h_ref.shape, 0)
        words = jnp.where(row < valid, h_ref[...], jnp.uint32(0))
        lo, hi = _unpack_pairs(words)
        h = jnp.concatenate([lo, hi], axis=1).astype(BF16)
        o_ref[...] = _pack_pairs(_swiglu_acc(h, w1_ref.at[0], w3_ref.at[0], w2_ref.at[0], nf))

    @pl.when(valid <= 0)
    def _():
        o_ref[...] = jnp.zeros_like(o_ref)


def _experts(h_sorted, tile_expert, tile_valid, w1, w3, w2, tile):
    n, dw = h_sorted.shape
    d, f = w1.shape[1:]
    grid_spec = pltpu.PrefetchScalarGridSpec(
        num_scalar_prefetch=2,
        grid=(n // tile,),
        in_specs=[
            pl.BlockSpec((tile, dw), lambda i, te, tv: (i, 0)),
            pl.BlockSpec((1, d, f), lambda i, te, tv: (te[i], 0, 0), pipeline_mode=pl.Buffered(1)),
            pl.BlockSpec((1, d, f), lambda i, te, tv: (te[i], 0, 0), pipeline_mode=pl.Buffered(1)),
            pl.BlockSpec((1, f, d), lambda i, te, tv: (te[i], 0, 0), pipeline_mode=pl.Buffered(1)),
        ],
        out_specs=pl.BlockSpec((tile, dw), lambda i, te, tv: (i, 0)),
    )
    return pl.pallas_call(
        functools.partial(_experts_kernel, 2),
        grid_spec=grid_spec,
        out_shape=jax.ShapeDtypeStruct((n, dw), jnp.uint32),
        compiler_params=_cparams("arbitrary"),
        name="experts",
    )(tile_expert, tile_valid, h_sorted, w1, w3, w2)


def _combine_kernel(x1_ref, ya_ref, yb_ref, rt_ref, mod_ref, fw_ref, o_ref):
    m = mod_ref[0]
    rt = rt_ref[0]
    a_lo, a_hi = _unpack_pairs(ya_ref[0])
    b_lo, b_hi = _unpack_pairs(yb_ref[0])
    p1, p2 = rt[:, 2:3], rt[:, 3:4]
    moe = jnp.concatenate([p1 * a_lo + p2 * b_lo, p1 * a_hi + p2 * b_hi], axis=1)
    o_ref[0] = _rms(x1_ref[0] + m[5:6] * moe) * fw_ref[...]


def _moe(x1, h2, route, mod_l, mod_row, w1, w3, w2, final_w, tm, tile):
    bsz, s, d = x1.shape
    n = bsz * s
    ex = route[..., :2].astype(jnp.int32).reshape(2 * n)
    onehot = (ex[:, None] == jnp.arange(N_EXPERTS, dtype=jnp.int32)[None, :]).astype(jnp.int32)
    csum = jnp.cumsum(onehot, axis=0)
    cnt = csum[-1]
    rank = jnp.sum((csum - onehot) * onehot, axis=1)
    padded = ((cnt + tile - 1) // tile) * tile
    ends = jnp.cumsum(padded)
    pos = (ends - padded)[ex] + rank
    n_rows = 2 * n + N_EXPERTS * tile
    tile_start = jnp.arange(n_rows // tile, dtype=jnp.int32) * tile
    tile_expert = jnp.minimum(jnp.searchsorted(ends, tile_start, side="right"), N_EXPERTS - 1).astype(jnp.int32)
    group_end = (ends - padded + cnt)[tile_expert]
    tile_valid = jnp.clip(group_end - tile_start, 0, tile).astype(jnp.int32)

    pos2 = pos.reshape(n, 2)
    dw = h2.shape[-1]
    h_sorted = _sc_scatter2(h2.reshape(n, dw), pos2[:, 0], pos2[:, 1], n_rows)
    y_sorted = _experts(h_sorted, tile_expert, tile_valid, w1, w3, w2, tile)
    ya, yb = _sc_gather2(y_sorted, pos2[:, 0], pos2[:, 1])
    ya = ya.reshape(bsz, s, dw)
    yb = yb.reshape(bsz, s, dw)
    tok = pl.BlockSpec((1, tm, d), lambda b, i: (b, i, 0))
    words = pl.BlockSpec((1, tm, dw), lambda b, i: (b, i, 0))
    return pl.pallas_call(
        _combine_kernel,
        grid=(bsz, s // tm),
        in_specs=[tok, words, words,
                  pl.BlockSpec((1, tm, LANE), lambda b, i: (b, i, 0)),
                  pl.BlockSpec((1, N_MOD, d), lambda b, i: (mod_row(b), 0, 0)),
                  _resident((1, d))],
        out_specs=tok,
        out_shape=jax.ShapeDtypeStruct((bsz, s, d), F32),
        compiler_params=_cparams("parallel", "arbitrary"),
        name="moe_combine",
    )(x1, ya, yb, route, mod_l, final_w[None])


def _pick_tile(s, pref):
    t = min(s, pref)
    while s % t:
        t //= 2
    return t


def kernel(x, c, ctx, c_ctx, mod_w, mod_b, norm1_w, norm2_w, w_in, ssd_conv_w, ssd_conv_b, ssd_dt_bias,
           ssd_a_log, ssd_d, ssd_norm_w, lru_conv_w, lru_conv_b, lru_rw, lru_rb, lru_iw, lru_ib, lru_lambda,
           lru_norm_w, w_out, ffn_w1, ffn_w3, ffn_w2, router_w, moe_w1, moe_w3, moe_w2, final_norm_w):
    depth = w_in.shape[0]
    bsz, seq, d = x.shape
    lc = ctx.shape[1]
    rows = seq // GRID_W
    heads = ssd_dt_bias.shape[-1]
    d_ssd = heads * SSD_HEADDIM
    d_xbc = ssd_conv_w.shape[-1]
    d_lru = lru_conv_w.shape[-1]
    splits = (d_ssd, d_xbc, d_lru, d_lru)
    off_dt = d_ssd + d_xbc
    off_lx = off_dt + 2 * heads
    perm = jnp.concatenate([jnp.arange(0, off_dt), jnp.arange(off_lx, off_lx + 2 * d_lru),
                            jnp.arange(off_dt, off_lx)])
    ctx_cols = 8
    ctx_rows = lc // ctx_cols
    lru_cw = 256

    mod = _modulation(c, c_ctx, mod_w, mod_b)
    lat_row = lambda b: b
    ctx_row = lambda b: bsz
    tm_lat = _pick_tile(seq, 512)
    tm_ctx = _pick_tile(lc, 512)
    tb_lat = _pick_tile(seq, 512)
    tb_ctx = _pick_tile(lc, 512)
    gw = d_ssd // SSD_GROUPS

    xc = ctx
    for l in range(depth):
        last = l == depth - 1
        mod_l = mod[l]
        w_perm = jnp.pad(w_in[l][:, perm], ((0, 0), (0, LANE - 2 * heads))).astype(BF16)
        ssd_p = dict(dt_bias=ssd_dt_bias[l], a_log=ssd_a_log[l], d_skip=ssd_d[l], norm_w=ssd_norm_w[l])
        lru_p = _lru_params(lru_conv_w[l], lru_conv_b[l], lru_rw[l], lru_rb[l], lru_iw[l], lru_ib[l],
                            lru_lambda[l], lru_cw)
        wo = w_out[l].astype(BF16)

        zc, xbcc, lxc, lgc, dtc = _inproj(xc, mod_l, ctx_row, norm1_w[l], w_perm, ssd_conv_w[l], ssd_conv_b[l],
                                          splits, 2 * heads, tm_ctx)
        zl, xbcl, lxl, lgl, dtl = _inproj(x, mod_l, lat_row, norm1_w[l], w_perm, ssd_conv_w[l], ssd_conv_b[l],
                                          splits, 2 * heads, tm_lat)

        zero_s = jnp.zeros((bsz, SSD_GROUPS, SSD_STATE, gw), F32)
        yc_ssd, s_f, s_b = _ssd_bidir(zc, xbcc, dtc, ssd_p, zero_s, zero_s, tb_ctx)
        yl_ssd, _, _ = _ssd_bidir(zl, xbcl, dtl, ssd_p, s_f, s_b, tb_lat)

        to_grid = lambda v: v.reshape(bsz, ctx_cols, ctx_rows, d_lru).transpose(0, 2, 1, 3)
        gc, h_fin = _lru(to_grid(lxc), to_grid(lgc), lru_p, jnp.zeros((bsz, 2, d_lru), F32), ctx_rows, True)
        gl, _ = _lru(lxl.reshape(bsz, rows, GRID_W, d_lru), lgl.reshape(bsz, rows, GRID_W, d_lru), lru_p,
                     h_fin, 8, False)
        gl = gl.reshape(bsz, seq, d_lru)

        if l % 2 == 0:
            w1, w3, w2 = (ffn_w1[l // 2].astype(BF16), ffn_w3[l // 2].astype(BF16), ffn_w2[l // 2].astype(BF16))
            x1, h2 = _outproj(yl_ssd, gl, x, mod_l, lat_row, lru_norm_w[l], wo, norm2_w[l], tm_lat)
            x = _ffn(x1, h2, mod_l, lat_row, w1, w3, w2, tm_lat, final_norm_w if last else None)
            if not last:
                gc = gc.transpose(0, 2, 1, 3).reshape(bsz, lc, d_lru)
                xc1, hc2 = _outproj(yc_ssd, gc, xc, mod_l, ctx_row, lru_norm_w[l], wo, norm2_w[l], tm_ctx)
                xc = _ffn(xc1, hc2, mod_l, ctx_row, w1, w3, w2, tm_ctx)
        else:
            w1, w3, w2 = (moe_w1[l // 2].astype(BF16), moe_w3[l // 2].astype(BF16), moe_w2[l // 2].astype(BF16))
            assert last, "a routed layer that is not the last layer is not implemented"
            x1, h2, route = _outproj(yl_ssd, gl, x, mod_l, lat_row, lru_norm_w[l], wo, norm2_w[l], tm_lat,
                                     router_w=router_w[l // 2])
            x = _moe(x1, h2, route, mod_l, lat_row, w1, w3, w2, final_norm_w, tm_lat, 512)
    return x
```

```python
import functools

import jax
import jax.numpy as jnp
from jax import lax
from jax.experimental import pallas as pl
from jax.experimental.pallas import tpu as pltpu
from jax.experimental.pallas import tpu_sc as plsc

F32 = jnp.float32
BF16 = jnp.bfloat16
HI = lax.Precision.HIGHEST

EPS = 1e-6
N_MOD = 6
GRID_W = 64
SSD_HEADDIM = 64
SSD_GROUPS = 2
SSD_STATE = 128
SSD_CHUNK = 128
CONV_K = 4
LRU_BW = 64
RGLRU_C = 8.0
N_EXPERTS = 8
LOG2E = 1.4426950408889634
F32_TINY = float(jnp.finfo(jnp.float32).tiny)
HALO = 16
LANE = 128
MOD_ROWS = 24
VMEM_LIMIT = 56 * 1024 * 1024
SC_CORES = 2
SC_WORKERS = 32
SC_WINDOW = 64


def _cparams(*sem):
    return pltpu.CompilerParams(dimension_semantics=sem, vmem_limit_bytes=VMEM_LIMIT)


def _resident(shape):
    nd = len(shape)
    return pl.BlockSpec(shape, lambda *_: (0,) * nd, pipeline_mode=pl.Buffered(1))


def _rms(x):
    return x * lax.rsqrt(jnp.mean(x * x, axis=-1, keepdims=True) + EPS)


def _pack_pairs(v):
    k = v.shape[1] // 2
    lo = lax.bitcast_convert_type(v[:, :k].astype(BF16).astype(F32), jnp.uint32)
    hi = lax.bitcast_convert_type(v[:, k:].astype(BF16).astype(F32), jnp.uint32)
    return hi | (lo >> 16)


def _unpack_pairs(w):
    lo = lax.bitcast_convert_type(w << 16, F32)
    hi = lax.bitcast_convert_type(w & jnp.uint32(0xFFFF0000), F32)
    return lo, hi


def _mod_kernel(s_ref, w_ref, b_ref, o_ref):
    s = s_ref[...]
    s = s * jax.nn.sigmoid(s)
    o_ref[0] = jnp.dot(s, w_ref[0], precision=HI, preferred_element_type=F32) + b_ref[0]


def _modulation(c, c_ctx, mod_w, mod_b):
    depth, d, n = mod_w.shape
    bsz = c.shape[0]
    assert bsz + 1 <= MOD_ROWS
    s = jnp.concatenate([c, c_ctx[None], jnp.zeros((MOD_ROWS - bsz - 1, d), F32)], axis=0)
    tn = n // 4
    out = pl.pallas_call(
        _mod_kernel,
        grid=(depth, n // tn),
        in_specs=[
            pl.BlockSpec((MOD_ROWS, d), lambda l, j: (0, 0)),
            pl.BlockSpec((1, d, tn), lambda l, j: (l, 0, j)),
            pl.BlockSpec((1, 1, tn), lambda l, j: (l, 0, j)),
        ],
        out_specs=pl.BlockSpec((1, MOD_ROWS, tn), lambda l, j: (l, 0, j)),
        out_shape=jax.ShapeDtypeStruct((depth, MOD_ROWS, n), F32),
        compiler_params=_cparams("arbitrary", "arbitrary"),
        name="modulation",
    )(s, mod_w, mod_b[:, None, :])
    return out.reshape(depth, MOD_ROWS, N_MOD, d)


def _inproj_kernel(splits, dt_w, nt, x_ref, xprev_ref, xnext_ref, mod_ref, nw_ref, w_ref, cw_ref, cb_ref,
                   z_ref, xs_ref, lx_ref, lg_ref, dt_ref, xe_ref):
    i = pl.program_id(1)
    m = mod_ref[0]
    tm = x_ref.shape[1]

    def norm_mod(v):
        return (_rms(v) * nw_ref[...] * (1.0 + m[1:2]) + m[0:1]).astype(BF16)

    h = norm_mod(x_ref[0])

    def proj(a, b):
        return jnp.dot(h, w_ref[:, a:b], preferred_element_type=F32)

    o = [0]
    for s in splits:
        o.append(o[-1] + s)
    h_ext = jnp.concatenate([norm_mod(xprev_ref[0, 0]), h, norm_mod(xnext_ref[0, 0])], axis=0)
    xe = jnp.dot(h_ext, w_ref[:, o[1]:o[2]], preferred_element_type=F32)
    row = lax.broadcasted_iota(jnp.int32, (tm + 2 * HALO, 1), 0)
    outside = jnp.logical_or(jnp.logical_and(row < HALO, i == 0),
                             jnp.logical_and(row >= tm + HALO, i == nt - 1))
    xe_ref[...] = jnp.where(outside, 0.0, xe)
    acc = cb_ref[...] + cw_ref[0:1] * xe_ref[HALO - 2:HALO - 2 + tm]
    for k in range(1, CONV_K):
        acc = acc + cw_ref[k:k + 1] * xe_ref[HALO - 2 + k:HALO - 2 + k + tm]
    xs_ref[0] = (acc * jax.nn.sigmoid(acc)).astype(BF16)

    z_ref[0] = proj(o[0], o[1]).astype(BF16)
    lx_ref[0] = proj(o[2], o[3]).astype(BF16)
    lg_ref[0] = proj(o[3], o[4]).astype(BF16)
    dt_ref[0] = proj(o[4], o[4] + LANE)[:, :dt_w]


def _inproj(x, mod_l, mod_row, norm_w, w_perm, conv_w, conv_b, splits, dt_w, tm):
    bsz, s, d = x.shape
    n_all = w_perm.shape[1]
    nt = s // tm
    nh = tm // HALO
    xh = x.reshape(bsz, s // HALO, HALO, d)
    outs = [jax.ShapeDtypeStruct((bsz, s, w), BF16) for w in splits] + [jax.ShapeDtypeStruct((bsz, s, dt_w), F32)]
    return pl.pallas_call(
        functools.partial(_inproj_kernel, splits, dt_w, nt),
        grid=(bsz, nt),
        in_specs=[
            pl.BlockSpec((1, tm, d), lambda b, i: (b, i, 0)),
            pl.BlockSpec((1, 1, HALO, d), lambda b, i: (b, jnp.maximum(i * nh - 1, 0), 0, 0)),
            pl.BlockSpec((1, 1, HALO, d), lambda b, i: (b, jnp.minimum((i + 1) * nh, s // HALO - 1), 0, 0)),
            pl.BlockSpec((1, N_MOD, d), lambda b, i: (mod_row(b), 0, 0)),
            _resident((1, d)),
            _resident((d, n_all)),
            _resident((CONV_K, splits[1])),
            _resident((1, splits[1])),
        ],
        out_specs=[pl.BlockSpec((1, tm, w), lambda b, i: (b, i, 0)) for w in splits]
        + [pl.BlockSpec((1, tm, dt_w), lambda b, i: (b, i, 0))],
        out_shape=outs,
        scratch_shapes=[pltpu.VMEM((tm + 2 * HALO, splits[1]), F32)],
        compiler_params=_cparams("parallel", "arbitrary"),
        name="inproj",
    )(x, xh, xh, mod_l, norm_w[None], w_perm, conv_w, conv_b[None])


def _softplus(x):
    return jnp.maximum(x, 0.0) + jnp.log1p(jnp.exp(-jnp.abs(x)))


def _ssd_kernel(rev, nblk, tb, heads, *refs):
    if rev:
        (xs_ref, dt_ref, dtb_ref, alog_ref, s0_ref,
         yf_ref, z_ref, dsk_ref, nw_ref, y_ref, sfin_ref, st_ref, yb_ref) = refs
    else:
        (xs_ref, dt_ref, dtb_ref, alog_ref, s0_ref, y_ref, sfin_ref, st_ref) = refs
        yb_ref = y_ref.at[0]
    i = pl.program_id(1)
    d_ssd = heads * SSD_HEADDIM
    gn = SSD_GROUPS * SSD_STATE
    gw = d_ssd // SSD_GROUPS
    ck = SSD_CHUNK

    @pl.when(i == 0)
    def _():
        st_ref[...] = s0_ref[0]

    d0 = heads if rev else 0
    dt = _softplus(dt_ref[0] + dtb_ref[...])[:, d0:d0 + heads]
    a_neg = -jnp.exp(alog_ref[...])[:, d0:d0 + heads]
    da = dt * a_neg
    ldt = jnp.where(dt > 0.0, jnp.log(dt), -1e30)

    ri = lax.broadcasted_iota(jnp.int32, (ck, ck), 0)
    ci = lax.broadcasted_iota(jnp.int32, (ck, ck), 1)
    tri = (ci >= ri) if rev else (ci <= ri)
    tri_f = tri.astype(F32)
    tri_t = ((ri >= ci) if rev else (ri <= ci)).astype(F32)
    eye = (ri == ci).astype(F32)
    rhs_rt = jnp.concatenate([tri_t, eye], axis=0)
    eh = lax.broadcasted_iota(jnp.int32, (heads, d_ssd), 0)
    ec = lax.broadcasted_iota(jnp.int32, (heads, d_ssd), 1)
    expand = jnp.where(ec // SSD_HEADDIM == eh, 1.0, 0.0).astype(BF16)
    expand2 = jnp.concatenate([expand, expand], axis=0)
    lane = lax.broadcasted_iota(jnp.int32, (ck, LANE), 1)
    end = 0 if rev else ck - 1

    def widen(q):
        hi = q.astype(BF16)
        lo = (q - hi.astype(F32)).astype(BF16)
        return jnp.dot(jnp.concatenate([hi, lo], axis=1), expand2, preferred_element_type=F32)

    order = range(tb // ck - 1, -1, -1) if rev else range(tb // ck)
    for c in order:
        r0 = c * ck
        x_bf = xs_ref[0, r0:r0 + ck, :d_ssd]
        b_bf = xs_ref[0, r0:r0 + ck, d_ssd:d_ssd + gn]
        c_bf = xs_ref[0, r0:r0 + ck, d_ssd + gn:]
        dac = da[r0:r0 + ck]
        cs = jnp.dot(tri_f, dac, precision=HI, preferred_element_type=F32)
        r_t = lax.dot_general(jnp.concatenate([dac, -ldt[r0:r0 + ck]], axis=0), rhs_rt,
                              (((0,), (0,)), ((), ())), precision=HI,
                              preferred_element_type=F32)
        ecs = jnp.exp(cs)
        tot = cs[end:end + 1]
        wgt = jnp.exp(tot - cs) * dt[r0:r0 + ck]
        ecs_w = widen(ecs)
        xw = (x_bf.astype(F32) * widen(wgt)).astype(BF16)
        cs2 = cs * LOG2E
        r_t2 = r_t * LOG2E
        for g in range(SSD_GROUPS):
            bg = b_bf[:, g * SSD_STATE:(g + 1) * SSD_STATE]
            cg = c_bf[:, g * SSD_STATE:(g + 1) * SSD_STATE]
            cb = lax.dot_general(cg, bg, (((1,), (1,)), ((), ())), preferred_element_type=F32)
            s_in = st_ref[g]
            y_off = jnp.dot(cg, s_in.astype(BF16), preferred_element_type=F32) * ecs_w[:, g * gw:(g + 1) * gw]
            new = lax.dot_general(bg, xw[:, g * gw:(g + 1) * gw], (((0,), (0,)), ((), ())),
                                  preferred_element_type=F32)
            st_ref[g] = ecs_w[end:end + 1, g * gw:(g + 1) * gw] * s_in + new
            for p in range(gw // LANE):
                ha = (g * gw + p * LANE) // SSD_HEADDIM
                lm = []
                for h in (ha, ha + 1):
                    seg2 = cs2[:, h:h + 1] - r_t2[h:h + 1, :]
                    lm.append(jnp.where(tri, jnp.exp2(seg2), 0.0) * cb)
                lhs = jnp.concatenate(lm, axis=1).astype(BF16)
                xpair = x_bf[:, g * gw + p * LANE:g * gw + (p + 1) * LANE]
                zero = jnp.zeros_like(xpair)
                rhs = jnp.concatenate([jnp.where(lane < SSD_HEADDIM, xpair, zero),
                                       jnp.where(lane >= SSD_HEADDIM, xpair, zero)], axis=0)
                col = g * gw + p * LANE
                yb_ref[r0:r0 + ck, col:col + LANE] = (
                    jnp.dot(lhs, rhs, preferred_element_type=F32) + y_off[:, p * LANE:(p + 1) * LANE]
                ).astype(yb_ref.dtype)

    if rev:
        z = z_ref[0].astype(F32)
        x_all = xs_ref[0, :, :d_ssd].astype(F32)
        tot_y = (yb_ref[...] + yf_ref[0].astype(F32) + dsk_ref[...] * x_all) * (z * jax.nn.sigmoid(z))
        y_ref[0] = (_rms(tot_y) * nw_ref[...]).astype(BF16)

    @pl.when(i == nblk - 1)
    def _():
        sfin_ref[0] = st_ref[...]


def _ssd_dir(rev, xs, dt_raw, dt_bias, a_log, s0, tb, extra=None):
    bsz, s, dxbc = xs.shape
    heads = dt_raw.shape[-1] // 2
    d_ssd = heads * SSD_HEADDIM
    gw = d_ssd // SSD_GROUPS
    nblk = s // tb
    pos = (lambda i: nblk - 1 - i) if rev else (lambda i: i)
    in_specs = [
        pl.BlockSpec((1, tb, dxbc), lambda b, i: (b, pos(i), 0)),
        pl.BlockSpec((1, tb, 2 * heads), lambda b, i: (b, pos(i), 0)),
        _resident((1, 2 * heads)),
        _resident((1, 2 * heads)),
        pl.BlockSpec((1, SSD_GROUPS, SSD_STATE, gw), lambda b, i: (b, 0, 0, 0)),
    ]
    args = [xs, dt_raw, dt_bias.reshape(1, -1), a_log.reshape(1, -1), s0]
    st_shape = jax.ShapeDtypeStruct((bsz, SSD_GROUPS, SSD_STATE, gw), F32)
    st_spec = pl.BlockSpec((1, SSD_GROUPS, SSD_STATE, gw), lambda b, i: (b, 0, 0, 0))
    scratch = [pltpu.VMEM((SSD_GROUPS, SSD_STATE, gw), F32)]
    if rev:
        yf, z, d_skip, norm_w = extra
        in_specs += [
            pl.BlockSpec((1, tb, d_ssd), lambda b, i: (b, pos(i), 0)),
            pl.BlockSpec((1, tb, d_ssd), lambda b, i: (b, pos(i), 0)),
            _resident((1, d_ssd)),
            _resident((1, d_ssd)),
        ]
        args += [yf, z, jnp.repeat(d_skip, SSD_HEADDIM)[None], norm_w[None]]
        y_dtype = BF16
        scratch.append(pltpu.VMEM((tb, d_ssd), F32))
    else:
        y_dtype = BF16
    return pl.pallas_call(
        functools.partial(_ssd_kernel, rev, nblk, tb, heads),
        grid=(bsz, nblk),
        in_specs=in_specs,
        out_specs=[pl.BlockSpec((1, tb, d_ssd), lambda b, i: (b, pos(i), 0)), st_spec],
        out_shape=[jax.ShapeDtypeStruct((bsz, s, d_ssd), y_dtype), st_shape],
        scratch_shapes=scratch,
        compiler_params=_cparams("parallel", "arbitrary"),
        name="ssd_bwd" if rev else "ssd_fwd",
    )(*args)


def _ssd_bidir(z, xs, dt_raw, p, s0_f, s0_b, tb):
    yf, s_f = _ssd_dir(False, xs, dt_raw, p["dt_bias"], p["a_log"], s0_f, tb)
    y, s_b = _ssd_dir(True, xs, dt_raw, p["dt_bias"], p["a_log"], s0_b, tb,
                      extra=(yf, z, p["d_skip"], p["norm_w"]))
    return y, s_f, s_b


def _shift_rows(x, s, fill, up):
    c = x.shape[0]
    rolled = pltpu.roll(x, (c - s) if up else s, 0)
    row = lax.broadcasted_iota(jnp.int32, x.shape, 0)
    keep = (row < c - s) if up else (row >= s)
    return jnp.where(keep, rolled, fill)


def _lru_kernel(nr, nc, rb, continuous, lx_ref, lg_ref, cw_ref, cb_ref, wg_ref, spl_ref, h0_ref,
                g_ref, hfin_ref, xp_ref, a_ref, b_ref):
    cwid = lx_ref.shape[-1]
    xin = lx_ref[0].astype(F32)
    xp_ref[2:nr + 2] = xin
    if continuous:
        for k in range(2):
            xp_ref[k] = _shift_rows(xin[nr - 2 + k], 1, 0.0, up=False)
        xp_ref[nr + 2] = _shift_rows(xin[0], 1, 0.0, up=True)
    else:
        xp_ref[0:2] = jnp.zeros((2, nc, cwid), F32)
        xp_ref[nr + 2:nr + 3] = jnp.zeros((1, nc, cwid), F32)

    def gates(t, _):
        r0 = pl.multiple_of(t * rb, rb)
        xc = cb_ref[...] + cw_ref[0:1] * xp_ref[pl.ds(r0, rb)]
        for k in range(1, CONV_K):
            xc = xc + cw_ref[k:k + 1] * xp_ref[pl.ds(r0 + k, rb)]
        xc2 = xc.reshape(rb * nc, cwid)
        x_aug = jnp.concatenate([xc2.astype(BF16), jnp.ones((rb * nc, LANE), BF16)], axis=1)
        gt = jnp.dot(x_aug, wg_ref[0], preferred_element_type=F32)
        x_half = 0.5 * xc2
        for d in range(2):
            t_r = jnp.tanh(gt[:, (2 * d) * cwid:(2 * d + 1) * cwid])
            t_i = jnp.tanh(gt[:, (2 * d + 1) * cwid:(2 * d + 2) * cwid])
            c_half = spl_ref[0, d:d + 1]
            nla = c_half * t_r + c_half
            a = jnp.exp2(nla * (-LOG2E))
            one_m_a2 = jnp.tanh(nla) * (a * a + 1.0)
            root = one_m_a2 * lax.rsqrt(jnp.maximum(one_m_a2, F32_TINY))
            a_ref[d, pl.ds(r0, rb)] = a.reshape(rb, nc, cwid)
            b_ref[d, pl.ds(r0, rb)] = (root * (t_i + 1.0) * x_half).reshape(rb, nc, cwid)
        return 0

    lax.fori_loop(0, nr // rb, gates, 0)

    hin = []
    for d in range(2):
        def totals(t, carry):
            h, p = carry
            r = (nr - 1 - t) if d else t
            a = a_ref[d, r]
            return a * h + b_ref[d, r], a * p

        h_l, p_l = lax.fori_loop(0, nr, totals, (jnp.zeros((nc, cwid), F32), jnp.ones((nc, cwid), F32)),
                                 unroll=4)
        sa, sb = p_l, h_l
        s = 1
        while s < nc:
            sb = sa * _shift_rows(sb, s, 0.0, up=bool(d)) + sb
            sa = sa * _shift_rows(sa, s, 1.0, up=bool(d))
            s *= 2
        h0 = h0_ref[0, d:d + 1]
        after = sa * h0 + sb
        hin.append(_shift_rows(after, 1, h0, up=bool(d)) if nc > 1 else jnp.broadcast_to(h0, (nc, cwid)))
        last = 0 if d else nc - 1
        hfin_ref[0, d:d + 1] = after[last:last + 1]

    def fwd(r, h):
        h = a_ref[0, r] * h + b_ref[0, r]
        b_ref[0, r] = h
        return h

    lax.fori_loop(0, nr, fwd, hin[0], unroll=4)

    def bwd(t, h):
        r = nr - 1 - t
        h = a_ref[1, r] * h + b_ref[1, r]
        g_ref[0, r] = ((h + b_ref[0, r]) * jax.nn.gelu(lg_ref[0, r].astype(F32))).astype(BF16)
        return h

    lax.fori_loop(0, nr, bwd, hin[1], unroll=4)


def _lru(lx, lg, p, h0, rb, continuous):
    bsz, nr, nc, d = lx.shape
    cwid = p["spl"].shape[-1]
    ng = d // cwid
    return pl.pallas_call(
        functools.partial(_lru_kernel, nr, nc, rb, continuous),
        grid=(bsz, ng),
        in_specs=[
            pl.BlockSpec((1, nr, nc, cwid), lambda b, j: (b, 0, 0, j)),
            pl.BlockSpec((1, nr, nc, cwid), lambda b, j: (b, 0, 0, j)),
            pl.BlockSpec((CONV_K, cwid), lambda b, j: (0, j)),
            pl.BlockSpec((1, cwid), lambda b, j: (0, j)),
            pl.BlockSpec((1, cwid + LANE, 4 * cwid), lambda b, j: (j, 0, 0)),
            pl.BlockSpec((1, 2, cwid), lambda b, j: (j, 0, 0)),
            pl.BlockSpec((1, 2, cwid), lambda b, j: (b, 0, j)),
        ],
        out_specs=[
            pl.BlockSpec((1, nr, nc, cwid), lambda b, j: (b, 0, 0, j)),
            pl.BlockSpec((1, 2, cwid), lambda b, j: (b, 0, j)),
        ],
        out_shape=[jax.ShapeDtypeStruct((bsz, nr, nc, d), BF16), jax.ShapeDtypeStruct((bsz, 2, d), F32)],
        scratch_shapes=[
            pltpu.VMEM((nr + 3, nc, cwid), F32),
            pltpu.VMEM((2, nr, nc, cwid), F32),
            pltpu.VMEM((2, nr, nc, cwid), F32),
        ],
        compiler_params=_cparams("parallel", "arbitrary"),
        name="rglru",
    )(lx, lg, p["conv_w"], p["conv_b"][None], p["wg"], p["spl"], h0)


def _lru_params(conv_w, conv_b, rw, rb_, iw, ib, lam, cwid):
    d = conv_w.shape[-1]
    ng = d // cwid
    nb = cwid // LRU_BW

    def block_diag(w):
        w = w.reshape(ng, nb, LRU_BW, LRU_BW)
        eye = jnp.eye(nb, dtype=w.dtype)
        return jnp.einsum("gakj,ab->gakbj", w, eye).reshape(ng, cwid, cwid)

    wg = jnp.concatenate([block_diag(rw[0]), block_diag(iw[0]), block_diag(rw[1]), block_diag(iw[1])], axis=-1)
    bg = 0.5 * jnp.concatenate([rb_[0].reshape(ng, cwid), ib[0].reshape(ng, cwid),
                                rb_[1].reshape(ng, cwid), ib[1].reshape(ng, cwid)], axis=-1)[:, None, :]
    bg_hi = bg.astype(BF16)
    bg_lo = (bg - bg_hi.astype(F32)).astype(BF16)
    wg_aug = jnp.concatenate([(0.5 * wg).astype(BF16), bg_hi, bg_lo,
                              jnp.zeros((ng, LANE - 2, 4 * cwid), BF16)], axis=1)
    spl = (0.5 * RGLRU_C) * jax.nn.softplus(-lam.astype(F32)).reshape(2, ng, cwid).transpose(1, 0, 2)
    return dict(conv_w=conv_w, conv_b=conv_b, wg=wg_aug, spl=spl)


def _swiglu_acc(h, w1_ref, w3_ref, w2_ref, nf):
    f = w1_ref.shape[-1]
    fc = f // nf
    acc = None
    for j in range(nf):
        a = jnp.dot(h, w1_ref[:, j * fc:(j + 1) * fc], preferred_element_type=F32)
        b = jnp.dot(h, w3_ref[:, j * fc:(j + 1) * fc], preferred_element_type=F32)
        pj = (a * jax.nn.sigmoid(a) * b).astype(BF16)
        o = jnp.dot(pj, w2_ref[j * fc:(j + 1) * fc, :], preferred_element_type=F32)
        acc = o if acc is None else acc + o
    return acc


def _outproj_kernel(route, final, y_ref, g_ref, x_ref, mod_ref, lnw_ref, wo_ref, n2w_ref, *rest):
    if route:
        rw_ref, x1_ref, h2_ref, rt_ref = rest
    elif final:
        w1_ref, w3_ref, w2_ref, fw_ref, o_ref = rest
    else:
        w1_ref, w3_ref, w2_ref, o_ref = rest
    m = mod_ref[0]
    gl = (_rms(g_ref[0].astype(F32)) * lnw_ref[...]).astype(BF16)
    dy = y_ref.shape[-1]
    proj = (jnp.dot(y_ref[0], wo_ref[0:dy], preferred_element_type=F32)
            + jnp.dot(gl, wo_ref[dy:], preferred_element_type=F32))
    x1 = x_ref[0] + m[2:3] * proj
    h2 = _rms(x1) * n2w_ref[...] * (1.0 + m[4:5]) + m[3:4]
    if not route:
        out = x1 + m[5:6] * _swiglu_acc(h2.astype(BF16), w1_ref, w3_ref, w2_ref, 2)
        if final:
            out = _rms(out) * fw_ref[...]
        o_ref[0] = out
        return
    x1_ref[0] = x1
    h2_ref[0] = _pack_pairs(h2)
    h_hi = h2.astype(BF16)
    h_lo = (h2 - h_hi.astype(F32)).astype(BF16)
    r_hi = jnp.dot(h_hi, rw_ref[...], preferred_element_type=F32)
    logits = (r_hi[:, :LANE] + r_hi[:, LANE:]
              + jnp.dot(h_lo, rw_ref[:, :LANE], preferred_element_type=F32))
    lane = lax.broadcasted_iota(jnp.int32, logits.shape, 1).astype(F32)
    neg = -jnp.inf
    logits = jnp.where(lane < N_EXPERTS, logits, neg)
    m1 = jnp.max(logits, axis=-1, keepdims=True)
    i1 = jnp.min(jnp.where(logits == m1, lane, float(LANE)), axis=-1, keepdims=True)
    rest_l = jnp.where(lane == i1, neg, logits)
    m2 = jnp.max(rest_l, axis=-1, keepdims=True)
    i2 = jnp.min(jnp.where(rest_l == m2, lane, float(LANE)), axis=-1, keepdims=True)
    e = jnp.exp(m2 - m1)
    p1 = 1.0 / (1.0 + e)
    p2 = e / (1.0 + e)
    rt_ref[0] = jnp.where(lane == 0.0, i1, jnp.where(lane == 1.0, i2,
                          jnp.where(lane == 2.0, p1, jnp.where(lane == 3.0, p2, 0.0))))


def _outproj(y, g, x, mod_l, mod_row, lru_norm_w, w_out, norm2_w, tm, router_w=None, ffn_w=None, final_w=None):
    bsz, s, d = x.shape
    route = router_w is not None
    final = final_w is not None and not route
    in_specs = [
        pl.BlockSpec((1, tm, y.shape[-1]), lambda b, i: (b, i, 0)),
        pl.BlockSpec((1, tm, g.shape[-1]), lambda b, i: (b, i, 0)),
        pl.BlockSpec((1, tm, d), lambda b, i: (b, i, 0)),
        pl.BlockSpec((1, N_MOD, d), lambda b, i: (mod_row(b), 0, 0)),
        _resident((1, g.shape[-1])),
        _resident(w_out.shape),
        _resident((1, d)),
    ]
    args = [y, g, x, mod_l, lru_norm_w[None], w_out, norm2_w[None]]
    tok = pl.BlockSpec((1, tm, d), lambda b, i: (b, i, 0))
    tok_shape = jax.ShapeDtypeStruct((bsz, s, d), F32)
    if route:
        rw = jnp.zeros((d, LANE), F32).at[:, :N_EXPERTS].set(router_w)
        rw_hi = rw.astype(BF16)
        in_specs.append(_resident((d, 2 * LANE)))
        args.append(jnp.concatenate([rw_hi, (rw - rw_hi.astype(F32)).astype(BF16)], axis=1))
        out_specs = [tok, pl.BlockSpec((1, tm, d // 2), lambda b, i: (b, i, 0)),
                     pl.BlockSpec((1, tm, LANE), lambda b, i: (b, i, 0))]
        out_shape = [tok_shape, jax.ShapeDtypeStruct((bsz, s, d // 2), jnp.uint32),
                     jax.ShapeDtypeStruct((bsz, s, LANE), F32)]
    else:
        in_specs += [_resident(w.shape) for w in ffn_w]
        args += list(ffn_w)
        if final:
            in_specs.append(_resident((1, d)))
            args.append(final_w[None])
        out_specs, out_shape = tok, tok_shape
    return pl.pallas_call(
        functools.partial(_outproj_kernel, route, final),
        grid=(bsz, s // tm),
        in_specs=in_specs,
        out_specs=out_specs,
        out_shape=out_shape,
        compiler_params=_cparams("parallel", "arbitrary"),
        name="outproj_route" if route else "outproj_ffn",
    )(*args)


def _sc_worker_id():
    return lax.axis_index("subcore") * SC_CORES + lax.axis_index("core")


def _sc_mesh():
    return plsc.VectorSubcoreMesh(core_axis_name="core", subcore_axis_name="subcore")


def _sc_scatter2(x, i0, i1, n_rows):
    m, d = x.shape
    per_w = m // SC_WORKERS
    assert per_w % SC_WINDOW == 0

    @functools.partial(
        pl.kernel, out_type=jax.ShapeDtypeStruct((n_rows, d), x.dtype), mesh=_sc_mesh(),
        scratch_types=[pltpu.VMEM((SC_WINDOW,), jnp.int32), pltpu.VMEM((SC_WINDOW, d), x.dtype),
                       pltpu.SemaphoreType.DMA])
    def scatter(x_hbm, i0_hbm, i1_hbm, o_hbm, idx_v, rows_v, sem):
        wid = _sc_worker_id()

        @pl.loop(0, per_w // SC_WINDOW)
        def _(j):
            base = wid * per_w + j * SC_WINDOW
            pltpu.sync_copy(x_hbm.at[pl.ds(base, SC_WINDOW)], rows_v)
            for i_hbm in (i0_hbm, i1_hbm):
                pltpu.sync_copy(i_hbm.at[pl.ds(base, SC_WINDOW)], idx_v)
                pltpu.async_copy(rows_v, o_hbm.at[idx_v], sem).wait()

    return scatter(x, i0, i1)


def _sc_gather2(table, i0, i1):
    m = i0.shape[0]
    d = table.shape[1]
    per_w = m // SC_WORKERS
    assert per_w % SC_WINDOW == 0
    out = jax.ShapeDtypeStruct((m, d), table.dtype)

    @functools.partial(
        pl.kernel, out_type=(out, out), mesh=_sc_mesh(),
        scratch_types=[pltpu.VMEM((SC_WINDOW,), jnp.int32), pltpu.VMEM((SC_WINDOW, d), table.dtype),
                       pltpu.SemaphoreType.DMA])
    def gather(t_hbm, i0_hbm, i1_hbm, a_hbm, b_hbm, idx_v, rows_v, sem):
        wid = _sc_worker_id()

        @pl.loop(0, per_w // SC_WINDOW)
        def _(j):
            base = wid * per_w + j * SC_WINDOW
            for i_hbm, o_hbm in ((i0_hbm, a_hbm), (i1_hbm, b_hbm)):
                pltpu.sync_copy(i_hbm.at[pl.ds(base, SC_WINDOW)], idx_v)
                pltpu.async_copy(t_hbm.at[idx_v], rows_v, sem).wait()
                pltpu.sync_copy(rows_v, o_hbm.at[pl.ds(base, SC_WINDOW)])

    return gather(table, i0, i1)


def _experts_kernel(nf, te_ref, tv_ref, h_ref, w1_ref, w3_ref, w2_ref, o_ref):
    valid = tv_ref[pl.program_id(0)]

    @pl.when(valid > 0)
    def _():
        row = lax.broadcasted_iota(jnp.int32, h_ref.shape, 0)
        words = jnp.where(row < valid, h_ref[...], jnp.uint32(0))
        lo, hi = _unpack_pairs(words)
        h = jnp.concatenate([lo, hi], axis=1).astype(BF16)
        o_ref[...] = _pack_pairs(_swiglu_acc(h, w1_ref.at[0], w3_ref.at[0], w2_ref.at[0], nf))

    @pl.when(valid <= 0)
    def _():
        o_ref[...] = jnp.zeros_like(o_ref)


def _experts(h_sorted, tile_expert, tile_valid, w1, w3, w2, tile):
    n, dw = h_sorted.shape
    d, f = w1.shape[1:]
    grid_spec = pltpu.PrefetchScalarGridSpec(
        num_scalar_prefetch=2,
        grid=(n // tile,),
        in_specs=[
            pl.BlockSpec((tile, dw), lambda i, te, tv: (i, 0)),
            pl.BlockSpec((1, d, f), lambda i, te, tv: (te[i], 0, 0), pipeline_mode=pl.Buffered(1)),
            pl.BlockSpec((1, d, f), lambda i, te, tv: (te[i], 0, 0), pipeline_mode=pl.Buffered(1)),
            pl.BlockSpec((1, f, d), lambda i, te, tv: (te[i], 0, 0), pipeline_mode=pl.Buffered(1)),
        ],
        out_specs=pl.BlockSpec((tile, dw), lambda i, te, tv: (i, 0)),
    )
    return pl.pallas_call(
        functools.partial(_experts_kernel, 2),
        grid_spec=grid_spec,
        out_shape=jax.ShapeDtypeStruct((n, dw), jnp.uint32),
        compiler_params=_cparams("arbitrary"),
        name="experts",
    )(tile_expert, tile_valid, h_sorted, w1, w3, w2)


def _combine_kernel(x1_ref, ya_ref, yb_ref, rt_ref, mod_ref, fw_ref, o_ref):
    m = mod_ref[0]
    rt = rt_ref[0]
    a_lo, a_hi = _unpack_pairs(ya_ref[0])
    b_lo, b_hi = _unpack_pairs(yb_ref[0])
    p1, p2 = rt[:, 2:3], rt[:, 3:4]
    moe = jnp.concatenate([p1 * a_lo + p2 * b_lo, p1 * a_hi + p2 * b_hi], axis=1)
    o_ref[0] = _rms(x1_ref[0] + m[5:6] * moe) * fw_ref[...]


def _moe(x1, h2, route, mod_l, mod_row, w1, w3, w2, final_w, tm, tile):
    bsz, s, d = x1.shape
    n = bsz * s
    ex = route[..., :2].astype(jnp.int32).reshape(2 * n)
    onehot = (ex[:, None] == jnp.arange(N_EXPERTS, dtype=jnp.int32)[None, :]).astype(jnp.int32)
    csum = jnp.cumsum(onehot, axis=0)
    cnt = csum[-1]
    rank = jnp.sum((csum - onehot) * onehot, axis=1)
    padded = ((cnt + tile - 1) // tile) * tile
    ends = jnp.cumsum(padded)
    pos = (ends - padded)[ex] + rank
    n_rows = 2 * n + N_EXPERTS * tile
    tile_start = jnp.arange(n_rows // tile, dtype=jnp.int32) * tile
    tile_expert = jnp.minimum(jnp.searchsorted(ends, tile_start, side="right"), N_EXPERTS - 1).astype(jnp.int32)
    group_end = (ends - padded + cnt)[tile_expert]
    tile_valid = jnp.clip(group_end - tile_start, 0, tile).astype(jnp.int32)

    pos2 = pos.reshape(n, 2)
    dw = h2.shape[-1]
    h_sorted = _sc_scatter2(h2.reshape(n, dw), pos2[:, 0], pos2[:, 1], n_rows)
    y_sorted = _experts(h_sorted, tile_expert, tile_valid, w1, w3, w2, tile)
    ya, yb = _sc_gather2(y_sorted, pos2[:, 0], pos2[:, 1])
    ya = ya.reshape(bsz, s, dw)
    yb = yb.reshape(bsz, s, dw)
    tok = pl.BlockSpec((1, tm, d), lambda b, i: (b, i, 0))
    words = pl.BlockSpec((1, tm, dw), lambda b, i: (b, i, 0))
    return pl.pallas_call(
        _combine_kernel,
        grid=(bsz, s // tm),
        in_specs=[tok, words, words,
                  pl.BlockSpec((1, tm, LANE), lambda b, i: (b, i, 0)),
                  pl.BlockSpec((1, N_MOD, d), lambda b, i: (mod_row(b), 0, 0)),
                  _resident((1, d))],
        out_specs=tok,
        out_shape=jax.ShapeDtypeStruct((bsz, s, d), F32),
        compiler_params=_cparams("parallel", "arbitrary"),
        name="moe_combine",
    )(x1, ya, yb, route, mod_l, final_w[None])


def _pick_tile(s, pref):
    t = min(s, pref)
    while s % t:
        t //= 2
    return t


def kernel(x, c, ctx, c_ctx, mod_w, mod_b, norm1_w, norm2_w, w_in, ssd_conv_w, ssd_conv_b, ssd_dt_bias,
           ssd_a_log, ssd_d, ssd_norm_w, lru_conv_w, lru_conv_b, lru_rw, lru_rb, lru_iw, lru_ib, lru_lambda,
           lru_norm_w, w_out, ffn_w1, ffn_w3, ffn_w2, router_w, moe_w1, moe_w3, moe_w2, final_norm_w):
    depth = w_in.shape[0]
    bsz, seq, d = x.shape
    lc = ctx.shape[1]
    rows = seq // GRID_W
    heads = ssd_dt_bias.shape[-1]
    d_ssd = heads * SSD_HEADDIM
    d_xbc = ssd_conv_w.shape[-1]
    d_lru = lru_conv_w.shape[-1]
    splits = (d_ssd, d_xbc, d_lru, d_lru)
    off_dt = d_ssd + d_xbc
    off_lx = off_dt + 2 * heads
    perm = jnp.concatenate([jnp.arange(0, off_dt), jnp.arange(off_lx, off_lx + 2 * d_lru),
                            jnp.arange(off_dt, off_lx)])
    ctx_cols = 8
    ctx_rows = lc // ctx_cols
    lru_cw = 256

    mod = _modulation(c, c_ctx, mod_w, mod_b)
    lat_row = lambda b: b
    ctx_row = lambda b: bsz
    tm_lat = _pick_tile(seq, 512)
    tm_ctx = _pick_tile(lc, 512)
    tb_lat = _pick_tile(seq, 512)
    tb_ctx = _pick_tile(lc, 512)
    gw = d_ssd // SSD_GROUPS

    xc = ctx
    for l in range(depth):
        last = l == depth - 1
        mod_l = mod[l]
        w_perm = jnp.pad(w_in[l][:, perm], ((0, 0), (0, LANE - 2 * heads))).astype(BF16)
        ssd_p = dict(dt_bias=ssd_dt_bias[l], a_log=ssd_a_log[l], d_skip=ssd_d[l], norm_w=ssd_norm_w[l])
        lru_p = _lru_params(lru_conv_w[l], lru_conv_b[l], lru_rw[l], lru_rb[l], lru_iw[l], lru_ib[l],
                            lru_lambda[l], lru_cw)
        wo = w_out[l].astype(BF16)

        zc, xbcc, lxc, lgc, dtc = _inproj(xc, mod_l, ctx_row, norm1_w[l], w_perm, ssd_conv_w[l], ssd_conv_b[l],
                                          splits, 2 * heads, tm_ctx)
        zl, xbcl, lxl, lgl, dtl = _inproj(x, mod_l, lat_row, norm1_w[l], w_perm, ssd_conv_w[l], ssd_conv_b[l],
                                          splits, 2 * heads, tm_lat)

        zero_s = jnp.zeros((bsz, SSD_GROUPS, SSD_STATE, gw), F32)
        yc_ssd, s_f, s_b = _ssd_bidir(zc, xbcc, dtc, ssd_p, zero_s, zero_s, tb_ctx)
        yl_ssd, _, _ = _ssd_bidir(zl, xbcl, dtl, ssd_p, s_f, s_b, tb_lat)

        to_grid = lambda v: v.reshape(bsz, ctx_cols, ctx_rows, d_lru).transpose(0, 2, 1, 3)
        gc, h_fin = _lru(to_grid(lxc), to_grid(lgc), lru_p, jnp.zeros((bsz, 2, d_lru), F32), ctx_rows, True)
        gl, _ = _lru(lxl.reshape(bsz, rows, GRID_W, d_lru), lgl.reshape(bsz, rows, GRID_W, d_lru), lru_p,
                     h_fin, 8, False)
        gl = gl.reshape(bsz, seq, d_lru)

        if l % 2 == 0:
            w1, w3, w2 = (ffn_w1[l // 2].astype(BF16), ffn_w3[l // 2].astype(BF16), ffn_w2[l // 2].astype(BF16))
            x_next = _outproj(yl_ssd, gl, x, mod_l, lat_row, lru_norm_w[l], wo, norm2_w[l], tm_lat,
                              ffn_w=(w1, w3, w2), final_w=final_norm_w if last else None)
            if not last:
                gc = gc.transpose(0, 2, 1, 3).reshape(bsz, lc, d_lru)
                xc = _outproj(yc_ssd, gc, xc, mod_l, ctx_row, lru_norm_w[l], wo, norm2_w[l], tm_ctx,
                              ffn_w=(w1, w3, w2))
            x = x_next
        else:
            w1, w3, w2 = (moe_w1[l // 2].astype(BF16), moe_w3[l // 2].astype(BF16), moe_w2[l // 2].astype(BF16))
            assert last, "a routed layer that is not the last layer is not implemented"
            x1, h2, route = _outproj(yl_ssd, gl, x, mod_l, lat_row, lru_norm_w[l], wo, norm2_w[l], tm_lat,
                                     router_w=router_w[l // 2])
            x = _moe(x1, h2, route, mod_l, lat_row, w1, w3, w2, final_norm_w, tm_lat, 512)
    return x
```

```python
import functools

import jax
import jax.numpy as jnp
from jax import lax
from jax.experimental import pallas as pl
from jax.experimental.pallas import tpu as pltpu
from jax.experimental.pallas import tpu_sc as plsc

F32 = jnp.float32
BF16 = jnp.bfloat16
HI = lax.Precision.HIGHEST

EPS = 1e-6
N_MOD = 6
GRID_W = 64
SSD_HEADDIM = 64
SSD_GROUPS = 2
SSD_STATE = 128
SSD_CHUNK = 128
CONV_K = 4
LRU_BW = 64
RGLRU_C = 8.0
N_EXPERTS = 8
LOG2E = 1.4426950408889634
F32_TINY = float(jnp.finfo(jnp.float32).tiny)
HALO = 16
LANE = 128
MOD_ROWS = 24
VMEM_LIMIT = 56 * 1024 * 1024
SC_CORES = 2
SC_WORKERS = 32
SC_WINDOW = 64


def _cparams(*sem):
    return pltpu.CompilerParams(dimension_semantics=sem, vmem_limit_bytes=VMEM_LIMIT)


def _resident(shape):
    nd = len(shape)
    return pl.BlockSpec(shape, lambda *_: (0,) * nd, pipeline_mode=pl.Buffered(1))


def _rms(x):
    return x * lax.rsqrt(jnp.mean(x * x, axis=-1, keepdims=True) + EPS)


def _pack_pairs(v):
    k = v.shape[1] // 2
    lo = lax.bitcast_convert_type(v[:, :k].astype(BF16).astype(F32), jnp.uint32)
    hi = lax.bitcast_convert_type(v[:, k:].astype(BF16).astype(F32), jnp.uint32)
    return hi | (lo >> 16)


def _unpack_pairs(w):
    lo = lax.bitcast_convert_type(w << 16, F32)
    hi = lax.bitcast_convert_type(w & jnp.uint32(0xFFFF0000), F32)
    return lo, hi


def _mod_kernel(s_ref, w_ref, b_ref, o_ref):
    s = s_ref[...]
    s = s * jax.nn.sigmoid(s)
    o_ref[0] = jnp.dot(s, w_ref[0], precision=HI, preferred_element_type=F32) + b_ref[0]


def _modulation(c, c_ctx, mod_w, mod_b):
    depth, d, n = mod_w.shape
    bsz = c.shape[0]
    assert bsz + 1 <= MOD_ROWS
    s = jnp.concatenate([c, c_ctx[None], jnp.zeros((MOD_ROWS - bsz - 1, d), F32)], axis=0)
    tn = n // 4
    out = pl.pallas_call(
        _mod_kernel,
        grid=(depth, n // tn),
        in_specs=[
            pl.BlockSpec((MOD_ROWS, d), lambda l, j: (0, 0)),
            pl.BlockSpec((1, d, tn), lambda l, j: (l, 0, j)),
            pl.BlockSpec((1, 1, tn), lambda l, j: (l, 0, j)),
        ],
        out_specs=pl.BlockSpec((1, MOD_ROWS, tn), lambda l, j: (l, 0, j)),
        out_shape=jax.ShapeDtypeStruct((depth, MOD_ROWS, n), F32),
        compiler_params=_cparams("arbitrary", "arbitrary"),
        name="modulation",
    )(s, mod_w, mod_b[:, None, :])
    return out.reshape(depth, MOD_ROWS, N_MOD, d)


def _inproj_kernel(splits, dt_w, nt, x_ref, xprev_ref, xnext_ref, mod_ref, nw_ref, w_ref, cw_ref, cb_ref,
                   z_ref, xs_ref, lx_ref, lg_ref, dt_ref):
    i = pl.program_id(1)
    m = mod_ref[0]
    tm = x_ref.shape[1]

    def norm_mod(v):
        return (_rms(v) * nw_ref[...] * (1.0 + m[1:2]) + m[0:1]).astype(BF16)

    h = norm_mod(x_ref[0])

    def proj(a, b):
        return jnp.dot(h, w_ref[:, a:b], preferred_element_type=F32)

    o = [0]
    for s in splits:
        o.append(o[-1] + s)
    h_ext = jnp.concatenate([norm_mod(xprev_ref[0, 0]), h, norm_mod(xnext_ref[0, 0])], axis=0)
    xe = jnp.dot(h_ext, w_ref[:, o[1]:o[2]], preferred_element_type=F32)
    row = lax.broadcasted_iota(jnp.int32, (tm + 2 * HALO, 1), 0)
    outside = jnp.logical_or(jnp.logical_and(row < HALO, i == 0),
                             jnp.logical_and(row >= tm + HALO, i == nt - 1))
    xe = jnp.where(outside, 0.0, xe)
    n_ext = tm + 2 * HALO
    acc = cb_ref[...] + cw_ref[2:3] * xe[HALO:HALO + tm]
    for k in (0, 1, 3):
        acc = acc + cw_ref[k:k + 1] * pltpu.roll(xe, (2 - k) % n_ext, 0)[HALO:HALO + tm]
    xs_ref[0] = (acc * jax.nn.sigmoid(acc)).astype(BF16)

    z_ref[0] = proj(o[0], o[1]).astype(BF16)
    lx_ref[0] = proj(o[2], o[3]).astype(BF16)
    lg_dt = proj(o[3], o[4] + LANE)
    lg_ref[0] = lg_dt[:, :splits[3]].astype(BF16)
    dt_ref[0] = lg_dt[:, splits[3]:splits[3] + dt_w]


def _inproj(x, mod_l, mod_row, norm_w, w_perm, conv_w, conv_b, splits, dt_w, tm):
    bsz, s, d = x.shape
    n_all = w_perm.shape[1]
    nt = s // tm
    nh = tm // HALO
    xh = x.reshape(bsz, s // HALO, HALO, d)
    outs = [jax.ShapeDtypeStruct((bsz, s, w), BF16) for w in splits] + [jax.ShapeDtypeStruct((bsz, s, dt_w), F32)]
    return pl.pallas_call(
        functools.partial(_inproj_kernel, splits, dt_w, nt),
        grid=(bsz, nt),
        in_specs=[
            pl.BlockSpec((1, tm, d), lambda b, i: (b, i, 0)),
            pl.BlockSpec((1, 1, HALO, d), lambda b, i: (b, jnp.maximum(i * nh - 1, 0), 0, 0)),
            pl.BlockSpec((1, 1, HALO, d), lambda b, i: (b, jnp.minimum((i + 1) * nh, s // HALO - 1), 0, 0)),
            pl.BlockSpec((1, N_MOD, d), lambda b, i: (mod_row(b), 0, 0)),
            _resident((1, d)),
            _resident((d, n_all)),
            _resident((CONV_K, splits[1])),
            _resident((1, splits[1])),
        ],
        out_specs=[pl.BlockSpec((1, tm, w), lambda b, i: (b, i, 0)) for w in splits]
        + [pl.BlockSpec((1, tm, dt_w), lambda b, i: (b, i, 0))],
        out_shape=outs,
        compiler_params=_cparams("parallel", "arbitrary"),
        name="inproj",
    )(x, xh, xh, mod_l, norm_w[None], w_perm, conv_w, conv_b[None])


def _softplus(x):
    return jnp.maximum(x, 0.0) + jnp.log1p(jnp.exp(-jnp.abs(x)))


def _ssd_kernel(rev, nblk, tb, heads, *refs):
    if rev:
        (xs_ref, dt_ref, dtb_ref, alog_ref, s0_ref,
         yf_ref, z_ref, dsk_ref, nw_ref, y_ref, sfin_ref, st_ref, yb_ref) = refs
    else:
        (xs_ref, dt_ref, dtb_ref, alog_ref, s0_ref, y_ref, sfin_ref, st_ref) = refs
        yb_ref = y_ref.at[0]
    i = pl.program_id(1)
    d_ssd = heads * SSD_HEADDIM
    gn = SSD_GROUPS * SSD_STATE
    gw = d_ssd // SSD_GROUPS
    ck = SSD_CHUNK

    @pl.when(i == 0)
    def _():
        st_ref[...] = s0_ref[0]

    d0 = heads if rev else 0
    dt = _softplus(dt_ref[0] + dtb_ref[...])[:, d0:d0 + heads]
    a_neg = -jnp.exp(alog_ref[...])[:, d0:d0 + heads]
    da = dt * a_neg
    ldt = jnp.where(dt > 0.0, jnp.log(dt), -1e30)

    ri = lax.broadcasted_iota(jnp.int32, (ck, ck), 0)
    ci = lax.broadcasted_iota(jnp.int32, (ck, ck), 1)
    tri = (ci >= ri) if rev else (ci <= ri)
    tri_b = jnp.where(tri, 1.0, 0.0).astype(BF16)
    tri_tb = jnp.where((ri >= ci) if rev else (ri <= ci), 1.0, 0.0).astype(BF16)
    eye_b = jnp.where(ri == ci, 1.0, 0.0).astype(BF16)

    def terms(q):
        hi = q.astype(BF16)
        r1 = q - hi.astype(F32)
        mid = r1.astype(BF16)
        return [hi, mid, (r1 - mid.astype(F32)).astype(BF16)]

    nck = tb // ck
    side = lambda q: jnp.concatenate([q[c * ck:(c + 1) * ck] for c in range(nck)], axis=1)
    da_s, ldt_s, dt_s = side(da), side(ldt), side(dt)
    cs_all = jnp.dot(jnp.concatenate([tri_b] * 3, axis=1), jnp.concatenate(terms(da_s), axis=0),
                     preferred_element_type=F32)
    rt_all = lax.dot_general(jnp.concatenate(terms(da_s) + terms(-ldt_s), axis=0),
                             jnp.concatenate([tri_tb] * 3 + [eye_b] * 3, axis=0),
                             (((0,), (0,)), ((), ())), preferred_element_type=F32)
    end = 0 if rev else ck - 1
    ecs_all = jnp.exp(cs_all)
    wgt_all = jnp.exp(cs_all[end:end + 1] - cs_all) * dt_s
    cs2_all = cs_all * LOG2E
    rt2_all = rt_all * LOG2E
    eh = lax.broadcasted_iota(jnp.int32, (heads, d_ssd), 0)
    ec = lax.broadcasted_iota(jnp.int32, (heads, d_ssd), 1)
    expand = jnp.where(ec // SSD_HEADDIM == eh, 1.0, 0.0).astype(BF16)
    expand2 = jnp.concatenate([expand, expand], axis=0)
    lane = lax.broadcasted_iota(jnp.int32, (ck, LANE), 1)

    def widen(q):
        hi = q.astype(BF16)
        lo = (q - hi.astype(F32)).astype(BF16)
        return jnp.dot(jnp.concatenate([hi, lo], axis=1), expand2, preferred_element_type=F32)

    order = range(tb // ck - 1, -1, -1) if rev else range(tb // ck)
    for c in order:
        r0 = c * ck
        x_bf = xs_ref[0, r0:r0 + ck, :d_ssd]
        b_bf = xs_ref[0, r0:r0 + ck, d_ssd:d_ssd + gn]
        c_bf = xs_ref[0, r0:r0 + ck, d_ssd + gn:]
        hs = slice(c * heads, (c + 1) * heads)
        ecs_w = widen(ecs_all[:, hs])
        xw = (x_bf.astype(F32) * widen(wgt_all[:, hs])).astype(BF16)
        cs2 = cs2_all[:, hs]
        r_t2 = rt2_all[hs, :]
        for g in range(SSD_GROUPS):
            bg = b_bf[:, g * SSD_STATE:(g + 1) * SSD_STATE]
            cg = c_bf[:, g * SSD_STATE:(g + 1) * SSD_STATE]
            cb = lax.dot_general(cg, bg, (((1,), (1,)), ((), ())), preferred_element_type=F32)
            s_in = st_ref[g]
            y_off = jnp.dot(cg, s_in.astype(BF16), preferred_element_type=F32) * ecs_w[:, g * gw:(g + 1) * gw]
            new = lax.dot_general(bg, xw[:, g * gw:(g + 1) * gw], (((0,), (0,)), ((), ())),
                                  preferred_element_type=F32)
            st_ref[g] = ecs_w[end:end + 1, g * gw:(g + 1) * gw] * s_in + new
            for p in range(gw // LANE):
                ha = (g * gw + p * LANE) // SSD_HEADDIM
                lm = []
                for h in (ha, ha + 1):
                    seg2 = cs2[:, h:h + 1] - r_t2[h:h + 1, :]
                    lm.append(jnp.where(tri, jnp.exp2(seg2), 0.0) * cb)
                lhs = jnp.concatenate(lm, axis=1).astype(BF16)
                xpair = x_bf[:, g * gw + p * LANE:g * gw + (p + 1) * LANE]
                zero = jnp.zeros_like(xpair)
                rhs = jnp.concatenate([jnp.where(lane < SSD_HEADDIM, xpair, zero),
                                       jnp.where(lane >= SSD_HEADDIM, xpair, zero)], axis=0)
                col = g * gw + p * LANE
                yb_ref[r0:r0 + ck, col:col + LANE] = (
                    jnp.dot(lhs, rhs, preferred_element_type=F32) + y_off[:, p * LANE:(p + 1) * LANE]
                ).astype(yb_ref.dtype)

    if rev:
        z = z_ref[0].astype(F32)
        x_all = xs_ref[0, :, :d_ssd].astype(F32)
        tot_y = (yb_ref[...] + yf_ref[0].astype(F32) + dsk_ref[...] * x_all) * (z * jax.nn.sigmoid(z))
        y_ref[0] = (_rms(tot_y) * nw_ref[...]).astype(BF16)

    @pl.when(i == nblk - 1)
    def _():
        sfin_ref[0] = st_ref[...]


def _ssd_dir(rev, xs, dt_raw, dt_bias, a_log, s0, tb, extra=None):
    bsz, s, dxbc = xs.shape
    heads = dt_raw.shape[-1] // 2
    d_ssd = heads * SSD_HEADDIM
    gw = d_ssd // SSD_GROUPS
    nblk = s // tb
    pos = (lambda i: nblk - 1 - i) if rev else (lambda i: i)
    in_specs = [
        pl.BlockSpec((1, tb, dxbc), lambda b, i: (b, pos(i), 0)),
        pl.BlockSpec((1, tb, 2 * heads), lambda b, i: (b, pos(i), 0)),
        _resident((1, 2 * heads)),
        _resident((1, 2 * heads)),
        pl.BlockSpec((1, SSD_GROUPS, SSD_STATE, gw), lambda b, i: (b, 0, 0, 0)),
    ]
    args = [xs, dt_raw, dt_bias.reshape(1, -1), a_log.reshape(1, -1), s0]
    st_shape = jax.ShapeDtypeStruct((bsz, SSD_GROUPS, SSD_STATE, gw), F32)
    st_spec = pl.BlockSpec((1, SSD_GROUPS, SSD_STATE, gw), lambda b, i: (b, 0, 0, 0))
    scratch = [pltpu.VMEM((SSD_GROUPS, SSD_STATE, gw), F32)]
    if rev:
        yf, z, d_skip, norm_w = extra
        in_specs += [
            pl.BlockSpec((1, tb, d_ssd), lambda b, i: (b, pos(i), 0)),
            pl.BlockSpec((1, tb, d_ssd), lambda b, i: (b, pos(i), 0)),
            _resident((1, d_ssd)),
            _resident((1, d_ssd)),
        ]
        args += [yf, z, jnp.repeat(d_skip, SSD_HEADDIM)[None], norm_w[None]]
        y_dtype = BF16
        scratch.append(pltpu.VMEM((tb, d_ssd), F32))
    else:
        y_dtype = BF16
    return pl.pallas_call(
        functools.partial(_ssd_kernel, rev, nblk, tb, heads),
        grid=(bsz, nblk),
        in_specs=in_specs,
        out_specs=[pl.BlockSpec((1, tb, d_ssd), lambda b, i: (b, pos(i), 0)), st_spec],
        out_shape=[jax.ShapeDtypeStruct((bsz, s, d_ssd), y_dtype), st_shape],
        scratch_shapes=scratch,
        compiler_params=_cparams("parallel", "arbitrary"),
        name="ssd_bwd" if rev else "ssd_fwd",
    )(*args)


def _ssd_bidir(z, xs, dt_raw, p, s0_f, s0_b, tb):
    yf, s_f = _ssd_dir(False, xs, dt_raw, p["dt_bias"], p["a_log"], s0_f, tb)
    y, s_b = _ssd_dir(True, xs, dt_raw, p["dt_bias"], p["a_log"], s0_b, tb,
                      extra=(yf, z, p["d_skip"], p["norm_w"]))
    return y, s_f, s_b


def _shift_rows(x, s, fill, up):
    c = x.shape[0]
    rolled = pltpu.roll(x, (c - s) if up else s, 0)
    row = lax.broadcasted_iota(jnp.int32, x.shape, 0)
    keep = (row < c - s) if up else (row >= s)
    return jnp.where(keep, rolled, fill)


def _lru_kernel(nr, nc, rb, continuous, lx_ref, lg_ref, cw_ref, cb_ref, wg_ref, bg_ref, spl_ref, h0_ref,
                g_ref, hfin_ref, xp_ref, a_ref, b_ref):
    cwid = lx_ref.shape[-1]
    xin = lx_ref[0].astype(F32)
    xp_ref[2:nr + 2] = xin
    if continuous:
        for k in range(2):
            xp_ref[k] = _shift_rows(xin[nr - 2 + k], 1, 0.0, up=False)
        xp_ref[nr + 2] = _shift_rows(xin[0], 1, 0.0, up=True)
    else:
        xp_ref[0:2] = jnp.zeros((2, nc, cwid), F32)
        xp_ref[nr + 2:nr + 3] = jnp.zeros((1, nc, cwid), F32)

    def gates(t, _):
        r0 = pl.multiple_of(t * rb, rb)
        xc = cb_ref[...] + cw_ref[0:1] * xp_ref[pl.ds(r0, rb)]
        for k in range(1, CONV_K):
            xc = xc + cw_ref[k:k + 1] * xp_ref[pl.ds(r0 + k, rb)]
        xc2 = xc.reshape(rb * nc, cwid)
        gt = jnp.dot(xc2.astype(BF16), wg_ref[0], preferred_element_type=F32) + bg_ref[0]
        x_half = 0.5 * xc2
        for d in range(2):
            t_r = jnp.tanh(gt[:, (2 * d) * cwid:(2 * d + 1) * cwid])
            t_i = jnp.tanh(gt[:, (2 * d + 1) * cwid:(2 * d + 2) * cwid])
            c_half = spl_ref[0, d:d + 1]
            nla = c_half * t_r + c_half
            a = jnp.exp2(nla * (-LOG2E))
            one_m_a2 = jnp.tanh(nla) * (a * a + 1.0)
            root = one_m_a2 * lax.rsqrt(jnp.maximum(one_m_a2, F32_TINY))
            a_ref[d, pl.ds(r0, rb)] = a.reshape(rb, nc, cwid)
            b_ref[d, pl.ds(r0, rb)] = (root * (t_i + 1.0) * x_half).reshape(rb, nc, cwid)
        return 0

    lax.fori_loop(0, nr // rb, gates, 0)

    hin = []
    for d in range(2):
        def totals(t, carry):
            h, p = carry
            r = (nr - 1 - t) if d else t
            a = a_ref[d, r]
            return a * h + b_ref[d, r], a * p

        h_l, p_l = lax.fori_loop(0, nr, totals, (jnp.zeros((nc, cwid), F32), jnp.ones((nc, cwid), F32)),
                                 unroll=4)
        sa, sb = p_l, h_l
        s = 1
        while s < nc:
            sb = sa * _shift_rows(sb, s, 0.0, up=bool(d)) + sb
            sa = sa * _shift_rows(sa, s, 1.0, up=bool(d))
            s *= 2
        h0 = h0_ref[0, d:d + 1]
        after = sa * h0 + sb
        hin.append(_shift_rows(after, 1, h0, up=bool(d)) if nc > 1 else jnp.broadcast_to(h0, (nc, cwid)))
        last = 0 if d else nc - 1
        hfin_ref[0, d:d + 1] = after[last:last + 1]

    def fwd(r, h):
        h = a_ref[0, r] * h + b_ref[0, r]
        b_ref[0, r] = h
        return h

    lax.fori_loop(0, nr, fwd, hin[0], unroll=4)

    def bwd(t, h):
        r = nr - 1 - t
        h = a_ref[1, r] * h + b_ref[1, r]
        g_ref[0, r] = ((h + b_ref[0, r]) * jax.nn.gelu(lg_ref[0, r].astype(F32))).astype(BF16)
        return h

    lax.fori_loop(0, nr, bwd, hin[1], unroll=4)


def _lru(lx, lg, p, h0, rb, continuous):
    bsz, nr, nc, d = lx.shape
    cwid = p["spl"].shape[-1]
    ng = d // cwid
    return pl.pallas_call(
        functools.partial(_lru_kernel, nr, nc, rb, continuous),
        grid=(bsz, ng),
        in_specs=[
            pl.BlockSpec((1, nr, nc, cwid), lambda b, j: (b, 0, 0, j)),
            pl.BlockSpec((1, nr, nc, cwid), lambda b, j: (b, 0, 0, j)),
            pl.BlockSpec((CONV_K, cwid), lambda b, j: (0, j)),
            pl.BlockSpec((1, cwid), lambda b, j: (0, j)),
            pl.BlockSpec((1, cwid, 4 * cwid), lambda b, j: (j, 0, 0)),
            pl.BlockSpec((1, 1, 4 * cwid), lambda b, j: (j, 0, 0)),
            pl.BlockSpec((1, 2, cwid), lambda b, j: (j, 0, 0)),
            pl.BlockSpec((1, 2, cwid), lambda b, j: (b, 0, j)),
        ],
        out_specs=[
            pl.BlockSpec((1, nr, nc, cwid), lambda b, j: (b, 0, 0, j)),
            pl.BlockSpec((1, 2, cwid), lambda b, j: (b, 0, j)),
        ],
        out_shape=[jax.ShapeDtypeStruct((bsz, nr, nc, d), BF16), jax.ShapeDtypeStruct((bsz, 2, d), F32)],
        scratch_shapes=[
            pltpu.VMEM((nr + 3, nc, cwid), F32),
            pltpu.VMEM((2, nr, nc, cwid), F32),
            pltpu.VMEM((2, nr, nc, cwid), F32),
        ],
        compiler_params=_cparams("parallel", "arbitrary"),
        name="rglru",
    )(lx, lg, p["conv_w"], p["conv_b"][None], p["wg"], p["bg"], p["spl"], h0)


def _lru_params(conv_w, conv_b, rw, rb_, iw, ib, lam, cwid):
    d = conv_w.shape[-1]
    ng = d // cwid
    nb = cwid // LRU_BW

    def block_diag(w):
        w = w.reshape(ng, nb, LRU_BW, LRU_BW)
        eye = jnp.eye(nb, dtype=w.dtype)
        return jnp.einsum("gakj,ab->gakbj", w, eye).reshape(ng, cwid, cwid)

    wg = jnp.concatenate([block_diag(rw[0]), block_diag(iw[0]), block_diag(rw[1]), block_diag(iw[1])], axis=-1)
    bg = 0.5 * jnp.concatenate([rb_[0].reshape(ng, cwid), ib[0].reshape(ng, cwid),
                                rb_[1].reshape(ng, cwid), ib[1].reshape(ng, cwid)], axis=-1)[:, None, :]
    spl = (0.5 * RGLRU_C) * jax.nn.softplus(-lam.astype(F32)).reshape(2, ng, cwid).transpose(1, 0, 2)
    return dict(conv_w=conv_w, conv_b=conv_b, wg=(0.5 * wg).astype(BF16), bg=bg, spl=spl)


def _swiglu_acc(h, w1_ref, w3_ref, w2_ref, nf):
    f = w1_ref.shape[-1]
    fc = f // nf
    acc = None
    for j in range(nf):
        a = jnp.dot(h, w1_ref[:, j * fc:(j + 1) * fc], preferred_element_type=F32)
        b = jnp.dot(h, w3_ref[:, j * fc:(j + 1) * fc], preferred_element_type=F32)
        pj = (a * jax.nn.sigmoid(a) * b).astype(BF16)
        o = jnp.dot(pj, w2_ref[j * fc:(j + 1) * fc, :], preferred_element_type=F32)
        acc = o if acc is None else acc + o
    return acc


def _outproj_kernel(route, final, y_ref, g_ref, x_ref, mod_ref, lnw_ref, wo_ref, n2w_ref, *rest):
    if route:
        rw_ref, x1_ref, h2_ref, rt_ref = rest
    elif final:
        w1_ref, w3_ref, w2_ref, fw_ref, o_ref = rest
    else:
        w1_ref, w3_ref, w2_ref, o_ref = rest
    m = mod_ref[0]
    gl = (_rms(g_ref[0].astype(F32)) * lnw_ref[...]).astype(BF16)
    dy = y_ref.shape[-1]
    proj = (jnp.dot(y_ref[0], wo_ref[0:dy], preferred_element_type=F32)
            + jnp.dot(gl, wo_ref[dy:], preferred_element_type=F32))
    x1 = x_ref[0] + m[2:3] * proj
    h2 = _rms(x1) * n2w_ref[...] * (1.0 + m[4:5]) + m[3:4]
    if not route:
        out = x1 + m[5:6] * _swiglu_acc(h2.astype(BF16), w1_ref, w3_ref, w2_ref, 2)
        if final:
            out = _rms(out) * fw_ref[...]
        o_ref[0] = out
        return
    x1_ref[0] = x1
    h2_ref[0] = _pack_pairs(h2)
    h_hi = h2.astype(BF16)
    h_lo = (h2 - h_hi.astype(F32)).astype(BF16)
    r_hi = jnp.dot(h_hi, rw_ref[...], preferred_element_type=F32)
    logits = (r_hi[:, :LANE] + r_hi[:, LANE:]
              + jnp.dot(h_lo, rw_ref[:, :LANE], preferred_element_type=F32))
    lane = lax.broadcasted_iota(jnp.int32, logits.shape, 1).astype(F32)
    neg = -jnp.inf
    logits = jnp.where(lane < N_EXPERTS, logits, neg)
    m1 = jnp.max(logits, axis=-1, keepdims=True)
    i1 = jnp.min(jnp.where(logits == m1, lane, float(LANE)), axis=-1, keepdims=True)
    rest_l = jnp.where(lane == i1, neg, logits)
    m2 = jnp.max(rest_l, axis=-1, keepdims=True)
    i2 = jnp.min(jnp.where(rest_l == m2, lane, float(LANE)), axis=-1, keepdims=True)
    e = jnp.exp(m2 - m1)
    p1 = 1.0 / (1.0 + e)
    p2 = e / (1.0 + e)
    rt_ref[0] = jnp.where(lane == 0.0, i1, jnp.where(lane == 1.0, i2,
                          jnp.where(lane == 2.0, p1, jnp.where(lane == 3.0, p2, 0.0))))


def _outproj(y, g, x, mod_l, mod_row, lru_norm_w, w_out, norm2_w, tm, router_w=None, ffn_w=None, final_w=None):
    bsz, s, d = x.shape
    route = router_w is not None
    final = final_w is not None and not route
    in_specs = [
        pl.BlockSpec((1, tm, y.shape[-1]), lambda b, i: (b, i, 0)),
        pl.BlockSpec((1, tm, g.shape[-1]), lambda b, i: (b, i, 0)),
        pl.BlockSpec((1, tm, d), lambda b, i: (b, i, 0)),
        pl.BlockSpec((1, N_MOD, d), lambda b, i: (mod_row(b), 0, 0)),
        _resident((1, g.shape[-1])),
        _resident(w_out.shape),
        _resident((1, d)),
    ]
    args = [y, g, x, mod_l, lru_norm_w[None], w_out, norm2_w[None]]
    tok = pl.BlockSpec((1, tm, d), lambda b, i: (b, i, 0))
    tok_shape = jax.ShapeDtypeStruct((bsz, s, d), F32)
    if route:
        rw = jnp.zeros((d, LANE), F32).at[:, :N_EXPERTS].set(router_w)
        rw_hi = rw.astype(BF16)
        in_specs.append(_resident((d, 2 * LANE)))
        args.append(jnp.concatenate([rw_hi, (rw - rw_hi.astype(F32)).astype(BF16)], axis=1))
        out_specs = [tok, pl.BlockSpec((1, tm, d // 2), lambda b, i: (b, i, 0)),
                     pl.BlockSpec((1, tm, LANE), lambda b, i: (b, i, 0))]
        out_shape = [tok_shape, jax.ShapeDtypeStruct((bsz, s, d // 2), jnp.uint32),
                     jax.ShapeDtypeStruct((bsz, s, LANE), F32)]
    else:
        in_specs += [_resident(w.shape) for w in ffn_w]
        args += list(ffn_w)
        if final:
            in_specs.append(_resident((1, d)))
            args.append(final_w[None])
        out_specs, out_shape = tok, tok_shape
    return pl.pallas_call(
        functools.partial(_outproj_kernel, route, final),
        grid=(bsz, s // tm),
        in_specs=in_specs,
        out_specs=out_specs,
        out_shape=out_shape,
        compiler_params=_cparams("parallel", "arbitrary"),
        name="outproj_route" if route else "outproj_ffn",
    )(*args)


def _sc_worker_id():
    return lax.axis_index("subcore") * SC_CORES + lax.axis_index("core")


def _sc_mesh():
    return plsc.VectorSubcoreMesh(core_axis_name="core", subcore_axis_name="subcore")


def _sc_scatter2(x, i0, i1, n_rows):
    m, d = x.shape
    per_w = m // SC_WORKERS
    assert per_w % SC_WINDOW == 0

    @functools.partial(
        pl.kernel, out_type=jax.ShapeDtypeStruct((n_rows, d), x.dtype), mesh=_sc_mesh(),
        scratch_types=[pltpu.VMEM((SC_WINDOW,), jnp.int32), pltpu.VMEM((SC_WINDOW, d), x.dtype),
                       pltpu.SemaphoreType.DMA])
    def scatter(x_hbm, i0_hbm, i1_hbm, o_hbm, idx_v, rows_v, sem):
        wid = _sc_worker_id()

        @pl.loop(0, per_w // SC_WINDOW)
        def _(j):
            base = wid * per_w + j * SC_WINDOW
            pltpu.sync_copy(x_hbm.at[pl.ds(base, SC_WINDOW)], rows_v)
            for i_hbm in (i0_hbm, i1_hbm):
                pltpu.sync_copy(i_hbm.at[pl.ds(base, SC_WINDOW)], idx_v)
                pltpu.async_copy(rows_v, o_hbm.at[idx_v], sem).wait()

    return scatter(x, i0, i1)


def _sc_gather2(table, i0, i1):
    m = i0.shape[0]
    d = table.shape[1]
    per_w = m // SC_WORKERS
    assert per_w % SC_WINDOW == 0
    out = jax.ShapeDtypeStruct((m, d), table.dtype)

    @functools.partial(
        pl.kernel, out_type=(out, out), mesh=_sc_mesh(),
        scratch_types=[pltpu.VMEM((SC_WINDOW,), jnp.int32), pltpu.VMEM((SC_WINDOW, d), table.dtype),
                       pltpu.SemaphoreType.DMA])
    def gather(t_hbm, i0_hbm, i1_hbm, a_hbm, b_hbm, idx_v, rows_v, sem):
        wid = _sc_worker_id()

        @pl.loop(0, per_w // SC_WINDOW)
        def _(j):
            base = wid * per_w + j * SC_WINDOW
            for i_hbm, o_hbm in ((i0_hbm, a_hbm), (i1_hbm, b_hbm)):
                pltpu.sync_copy(i_hbm.at[pl.ds(base, SC_WINDOW)], idx_v)
                pltpu.async_copy(t_hbm.at[idx_v], rows_v, sem).wait()
                pltpu.sync_copy(rows_v, o_hbm.at[pl.ds(base, SC_WINDOW)])

    return gather(table, i0, i1)


def _experts_kernel(nf, te_ref, tv_ref, h_ref, w1_ref, w3_ref, w2_ref, o_ref):
    valid = tv_ref[pl.program_id(0)]

    @pl.when(valid > 0)
    def _():
        row = lax.broadcasted_iota(jnp.int32, h_ref.shape, 0)
        words = jnp.where(row < valid, h_ref[...], jnp.uint32(0))
        lo, hi = _unpack_pairs(words)
        h = jnp.concatenate([lo, hi], axis=1).astype(BF16)
        o_ref[...] = _pack_pairs(_swiglu_acc(h, w1_ref.at[0], w3_ref.at[0], w2_ref.at[0], nf))

    @pl.when(valid <= 0)
    def _():
        o_ref[...] = jnp.zeros_like(o_ref)


def _experts(h_sorted, tile_expert, tile_valid, w1, w3, w2, tile):
    n, dw = h_sorted.shape
    d, f = w1.shape[1:]
    grid_spec = pltpu.PrefetchScalarGridSpec(
        num_scalar_prefetch=2,
        grid=(n // tile,),
        in_specs=[
            pl.BlockSpec((tile, dw), lambda i, te, tv: (i, 0)),
            pl.BlockSpec((1, d, f), lambda i, te, tv: (te[i], 0, 0), pipeline_mode=pl.Buffered(1)),
            pl.BlockSpec((1, d, f), lambda i, te, tv: (te[i], 0, 0), pipeline_mode=pl.Buffered(1)),
            pl.BlockSpec((1, f, d), lambda i, te, tv: (te[i], 0, 0), pipeline_mode=pl.Buffered(1)),
        ],
        out_specs=pl.BlockSpec((tile, dw), lambda i, te, tv: (i, 0)),
    )
    return pl.pallas_call(
        functools.partial(_experts_kernel, 2),
        grid_spec=grid_spec,
        out_shape=jax.ShapeDtypeStruct((n, dw), jnp.uint32),
        compiler_params=_cparams("arbitrary"),
        name="experts",
    )(tile_expert, tile_valid, h_sorted, w1, w3, w2)


def _combine_kernel(x1_ref, ya_ref, yb_ref, rt_ref, mod_ref, fw_ref, o_ref):
    m = mod_ref[0]
    rt = rt_ref[0]
    a_lo, a_hi = _unpack_pairs(ya_ref[0])
    b_lo, b_hi = _unpack_pairs(yb_ref[0])
    p1, p2 = rt[:, 2:3], rt[:, 3:4]
    moe = jnp.concatenate([p1 * a_lo + p2 * b_lo, p1 * a_hi + p2 * b_hi], axis=1)
    o_ref[0] = _rms(x1_ref[0] + m[5:6] * moe) * fw_ref[...]


def _moe(x1, h2, route, mod_l, mod_row, w1, w3, w2, final_w, tm, tile):
    bsz, s, d = x1.shape
    n = bsz * s
    ex = route[..., :2].astype(jnp.int32).reshape(2 * n)
    onehot = (ex[:, None] == jnp.arange(N_EXPERTS, dtype=jnp.int32)[None, :]).astype(jnp.int32)
    csum = jnp.cumsum(onehot, axis=0)
    cnt = csum[-1]
    rank = jnp.sum((csum - onehot) * onehot, axis=1)
    padded = ((cnt + tile - 1) // tile) * tile
    ends = jnp.cumsum(padded)
    pos = (ends - padded)[ex] + rank
    n_rows = 2 * n + N_EXPERTS * tile
    tile_start = jnp.arange(n_rows // tile, dtype=jnp.int32) * tile
    tile_expert = jnp.minimum(jnp.searchsorted(ends, tile_start, side="right"), N_EXPERTS - 1).astype(jnp.int32)
    group_end = (ends - padded + cnt)[tile_expert]
    tile_valid = jnp.clip(group_end - tile_start, 0, tile).astype(jnp.int32)

    pos2 = pos.reshape(n, 2)
    dw = h2.shape[-1]
    h_sorted = _sc_scatter2(h2.reshape(n, dw), pos2[:, 0], pos2[:, 1], n_rows)
    y_sorted = _experts(h_sorted, tile_expert, tile_valid, w1, w3, w2, tile)
    ya, yb = _sc_gather2(y_sorted, pos2[:, 0], pos2[:, 1])
    ya = ya.reshape(bsz, s, dw)
    yb = yb.reshape(bsz, s, dw)
    tok = pl.BlockSpec((1, tm, d), lambda b, i: (b, i, 0))
    words = pl.BlockSpec((1, tm, dw), lambda b, i: (b, i, 0))
    return pl.pallas_call(
        _combine_kernel,
        grid=(bsz, s // tm),
        in_specs=[tok, words, words,
                  pl.BlockSpec((1, tm, LANE), lambda b, i: (b, i, 0)),
                  pl.BlockSpec((1, N_MOD, d), lambda b, i: (mod_row(b), 0, 0)),
                  _resident((1, d))],
        out_specs=tok,
        out_shape=jax.ShapeDtypeStruct((bsz, s, d), F32),
        compiler_params=_cparams("parallel", "arbitrary"),
        name="moe_combine",
    )(x1, ya, yb, route, mod_l, final_w[None])


def _pick_tile(s, pref):
    t = min(s, pref)
    while s % t:
        t //= 2
    return t


def kernel(x, c, ctx, c_ctx, mod_w, mod_b, norm1_w, norm2_w, w_in, ssd_conv_w, ssd_conv_b, ssd_dt_bias,
           ssd_a_log, ssd_d, ssd_norm_w, lru_conv_w, lru_conv_b, lru_rw, lru_rb, lru_iw, lru_ib, lru_lambda,
           lru_norm_w, w_out, ffn_w1, ffn_w3, ffn_w2, router_w, moe_w1, moe_w3, moe_w2, final_norm_w):
    depth = w_in.shape[0]
    bsz, seq, d = x.shape
    lc = ctx.shape[1]
    rows = seq // GRID_W
    heads = ssd_dt_bias.shape[-1]
    d_ssd = heads * SSD_HEADDIM
    d_xbc = ssd_conv_w.shape[-1]
    d_lru = lru_conv_w.shape[-1]
    splits = (d_ssd, d_xbc, d_lru, d_lru)
    off_dt = d_ssd + d_xbc
    off_lx = off_dt + 2 * heads
    perm = jnp.concatenate([jnp.arange(0, off_dt), jnp.arange(off_lx, off_lx + 2 * d_lru),
                            jnp.arange(off_dt, off_lx)])
    ctx_cols = 8
    ctx_rows = lc // ctx_cols
    lru_cw = 256

    mod = _modulation(c, c_ctx, mod_w, mod_b)
    lat_row = lambda b: b
    ctx_row = lambda b: bsz
    tm_lat = _pick_tile(seq, 512)
    tm_ctx = _pick_tile(lc, 512)
    tb_lat = _pick_tile(seq, 512)
    tb_ctx = _pick_tile(lc, 512)
    gw = d_ssd // SSD_GROUPS

    xc = ctx
    for l in range(depth):
        last = l == depth - 1
        mod_l = mod[l]
        w_perm = jnp.pad(w_in[l][:, perm], ((0, 0), (0, LANE - 2 * heads))).astype(BF16)
        ssd_p = dict(dt_bias=ssd_dt_bias[l], a_log=ssd_a_log[l], d_skip=ssd_d[l], norm_w=ssd_norm_w[l])
        lru_p = _lru_params(lru_conv_w[l], lru_conv_b[l], lru_rw[l], lru_rb[l], lru_iw[l], lru_ib[l],
                            lru_lambda[l], lru_cw)
        wo = w_out[l].astype(BF16)

        zc, xbcc, lxc, lgc, dtc = _inproj(xc, mod_l, ctx_row, norm1_w[l], w_perm, ssd_conv_w[l], ssd_conv_b[l],
                                          splits, 2 * heads, tm_ctx)
        zl, xbcl, lxl, lgl, dtl = _inproj(x, mod_l, lat_row, norm1_w[l], w_perm, ssd_conv_w[l], ssd_conv_b[l],
                                          splits, 2 * heads, tm_lat)

        zero_s = jnp.zeros((bsz, SSD_GROUPS, SSD_STATE, gw), F32)
        yc_ssd, s_f, s_b = _ssd_bidir(zc, xbcc, dtc, ssd_p, zero_s, zero_s, tb_ctx)
        yl_ssd, _, _ = _ssd_bidir(zl, xbcl, dtl, ssd_p, s_f, s_b, tb_lat)

        to_grid = lambda v: v.reshape(bsz, ctx_cols, ctx_rows, d_lru).transpose(0, 2, 1, 3)
        gc, h_fin = _lru(to_grid(lxc), to_grid(lgc), lru_p, jnp.zeros((bsz, 2, d_lru), F32), ctx_rows, True)
        gl, _ = _lru(lxl.reshape(bsz, rows, GRID_W, d_lru), lgl.reshape(bsz, rows, GRID_W, d_lru), lru_p,
                     h_fin, 8, False)
        gl = gl.reshape(bsz, seq, d_lru)

        if l % 2 == 0:
            w1, w3, w2 = (ffn_w1[l // 2].astype(BF16), ffn_w3[l // 2].astype(BF16), ffn_w2[l // 2].astype(BF16))
            x_next = _outproj(yl_ssd, gl, x, mod_l, lat_row, lru_norm_w[l], wo, norm2_w[l], tm_lat,
                              ffn_w=(w1, w3, w2), final_w=final_norm_w if last else None)
            if not last:
                gc = gc.transpose(0, 2, 1, 3).reshape(bsz, lc, d_lru)
                xc = _outproj(yc_ssd, gc, xc, mod_l, ctx_row, lru_norm_w[l], wo, norm2_w[l], tm_ctx,
                              ffn_w=(w1, w3, w2))
            x = x_next
        else:
            w1, w3, w2 = (moe_w1[l // 2].astype(BF16), moe_w3[l // 2].astype(BF16), moe_w2[l // 2].astype(BF16))
            assert last, "a routed layer that is not the last layer is not implemented"
            x1, h2, route = _outproj(yl_ssd, gl, x, mod_l, lat_row, lru_norm_w[l], wo, norm2_w[l], tm_lat,
                                     router_w=router_w[l // 2])
            x = _moe(x1, h2, route, mod_l, lat_row, w1, w3, w2, final_norm_w, tm_lat, 512)
    return x
```

```python
import functools

import jax
import jax.numpy as jnp
from jax import lax
from jax.experimental import pallas as pl
from jax.experimental.pallas import tpu as pltpu
from jax.experimental.pallas import tpu_sc as plsc

F32 = jnp.float32
BF16 = jnp.bfloat16
HI = lax.Precision.HIGHEST

EPS = 1e-6
N_MOD = 6
GRID_W = 64
SSD_HEADDIM = 64
SSD_GROUPS = 2
SSD_STATE = 128
SSD_CHUNK = 128
CONV_K = 4
LRU_BW = 64
RGLRU_C = 8.0
N_EXPERTS = 8
LOG2E = 1.4426950408889634
GELU_K = 0.7978845608028654
F32_TINY = float(jnp.finfo(jnp.float32).tiny)
HALO = 16
LANE = 128
MXU_DEPTH = 256
MOD_ROWS = 24
VMEM_LIMIT = 56 * 1024 * 1024
SC_CORES = 2
SC_WORKERS = 32
SC_WINDOW = 128


def _cparams(*sem):
    return pltpu.CompilerParams(dimension_semantics=sem, vmem_limit_bytes=VMEM_LIMIT)


def _resident(shape):
    nd = len(shape)
    return pl.BlockSpec(shape, lambda *_: (0,) * nd, pipeline_mode=pl.Buffered(1))


def _rms(x):
    return x * lax.rsqrt(jnp.mean(x * x, axis=-1, keepdims=True) + EPS)


def _pack_pairs(v):
    k = v.shape[1] // 2
    lo = lax.bitcast_convert_type(v[:, :k].astype(BF16).astype(F32), jnp.uint32)
    hi = lax.bitcast_convert_type(v[:, k:].astype(BF16).astype(F32), jnp.uint32)
    return hi | (lo >> 16)


def _unpack_pairs(w):
    lo = lax.bitcast_convert_type(w << 16, F32)
    hi = lax.bitcast_convert_type(w & jnp.uint32(0xFFFF0000), F32)
    return lo, hi


def _mod_kernel(s_ref, w_ref, b_ref, o_ref):
    s = s_ref[...]
    s = s * jax.nn.sigmoid(s)
    o_ref[0] = jnp.dot(s, w_ref[0], precision=HI, preferred_element_type=F32) + b_ref[0]


def _modulation(c, c_ctx, mod_w, mod_b):
    depth, d, n = mod_w.shape
    bsz = c.shape[0]
    assert bsz + 1 <= MOD_ROWS
    s = jnp.concatenate([c, c_ctx[None], jnp.zeros((MOD_ROWS - bsz - 1, d), F32)], axis=0)
    tn = n // 4
    out = pl.pallas_call(
        _mod_kernel,
        grid=(depth, n // tn),
        in_specs=[
            pl.BlockSpec((MOD_ROWS, d), lambda l, j: (0, 0)),
            pl.BlockSpec((1, d, tn), lambda l, j: (l, 0, j)),
            pl.BlockSpec((1, 1, tn), lambda l, j: (l, 0, j)),
        ],
        out_specs=pl.BlockSpec((1, MOD_ROWS, tn), lambda l, j: (l, 0, j)),
        out_shape=jax.ShapeDtypeStruct((depth, MOD_ROWS, n), F32),
        compiler_params=_cparams("arbitrary", "arbitrary"),
        name="modulation",
    )(s, mod_w, mod_b[:, None, :])
    return out.reshape(depth, MOD_ROWS, N_MOD, d)


def _inproj_kernel(splits, dt_w, nt, x_ref, xprev_ref, xnext_ref, mod_ref, nw_ref, w_ref, cw_ref, cb_ref,
                   z_ref, xs_ref, lx_ref, lg_ref, dt_ref):
    i = pl.program_id(1)
    m = mod_ref[0]
    tm = x_ref.shape[1]

    def norm_mod(v):
        return (_rms(v) * nw_ref[...] * (1.0 + m[1:2]) + m[0:1]).astype(BF16)

    h = norm_mod(x_ref[0])

    def proj(a, b):
        return jnp.dot(h, w_ref[:, a:b], preferred_element_type=F32)

    o = [0]
    for s in splits:
        o.append(o[-1] + s)
    h_ext = jnp.concatenate([norm_mod(xprev_ref[0, 0]), h, norm_mod(xnext_ref[0, 0])], axis=0)
    xe = jnp.dot(h_ext, w_ref[:, o[1]:o[2]], preferred_element_type=F32)
    row = lax.broadcasted_iota(jnp.int32, (tm + 2 * HALO, 1), 0)
    outside = jnp.logical_or(jnp.logical_and(row < HALO, i == 0),
                             jnp.logical_and(row >= tm + HALO, i == nt - 1))
    xe = jnp.where(outside, 0.0, xe)
    n_ext = tm + 2 * HALO

    acc = cb_ref[...] + cw_ref[2:3] * xe[HALO:HALO + tm]
    for k in (0, 1, 3):
        acc = acc + cw_ref[k:k + 1] * pltpu.roll(xe, (2 - k) % n_ext, 0)[HALO:HALO + tm]
    xs_ref[0] = (acc * jax.nn.sigmoid(acc)).astype(BF16)

    z_ref[0] = proj(o[0], o[1]).astype(BF16)
    lx_ref[0] = proj(o[2], o[3]).astype(BF16)
    lg_dt = proj(o[3], o[4] + LANE)
    lg_ref[0] = lg_dt[:, :splits[3]].astype(BF16)
    dt_ref[0] = lg_dt[:, splits[3]:splits[3] + dt_w]


def _inproj(x, mod_l, mod_row, norm_w, w_perm, conv_w, conv_b, splits, dt_w, tm):
    bsz, s, d = x.shape
    n_all = w_perm.shape[1]
    nt = s // tm
    nh = tm // HALO
    xh = x.reshape(bsz, s // HALO, HALO, d)
    outs = [jax.ShapeDtypeStruct((bsz, s, w), BF16) for w in splits] + [jax.ShapeDtypeStruct((bsz, s, dt_w), F32)]
    return pl.pallas_call(
        functools.partial(_inproj_kernel, splits, dt_w, nt),
        grid=(bsz, nt),
        in_specs=[
            pl.BlockSpec((1, tm, d), lambda b, i: (b, i, 0)),
            pl.BlockSpec((1, 1, HALO, d), lambda b, i: (b, jnp.maximum(i * nh - 1, 0), 0, 0)),
            pl.BlockSpec((1, 1, HALO, d), lambda b, i: (b, jnp.minimum((i + 1) * nh, s // HALO - 1), 0, 0)),
            pl.BlockSpec((1, N_MOD, d), lambda b, i: (mod_row(b), 0, 0)),
            _resident((1, d)),
            _resident((d, n_all)),
            _resident((CONV_K, splits[1])),
            _resident((1, splits[1])),
        ],
        out_specs=[pl.BlockSpec((1, tm, w), lambda b, i: (b, i, 0)) for w in splits]
        + [pl.BlockSpec((1, tm, dt_w), lambda b, i: (b, i, 0))],
        out_shape=outs,
        compiler_params=_cparams("parallel", "arbitrary"),
        name="inproj",
    )(x, xh, xh, mod_l, norm_w[None], w_perm, conv_w, conv_b[None])


def _softplus(x):
    return jnp.maximum(x, 0.0) + jnp.log1p(jnp.exp(-jnp.abs(x)))


def _ssd_kernel(rev, nblk, tb, heads, *refs):
    if rev:
        (xs_ref, dt_ref, dtb_ref, alog_ref, s0_ref,
         yf_ref, z_ref, dsk_ref, nw_ref, y_ref, sfin_ref, st_ref, yb_ref) = refs
    else:
        (xs_ref, dt_ref, dtb_ref, alog_ref, s0_ref, y_ref, sfin_ref, st_ref) = refs
        yb_ref = y_ref.at[0]
    i = pl.program_id(1)
    d_ssd = heads * SSD_HEADDIM
    gn = SSD_GROUPS * SSD_STATE
    gw = d_ssd // SSD_GROUPS
    ck = SSD_CHUNK

    @pl.when(i == 0)
    def _():
        st_ref[...] = s0_ref[0]

    d0 = heads if rev else 0
    dt = _softplus(dt_ref[0] + dtb_ref[...])[:, d0:d0 + heads]
    a_neg = -jnp.exp(alog_ref[...])[:, d0:d0 + heads]
    da = dt * a_neg
    ldt = jnp.where(dt > 0.0, jnp.log(dt), -1e30)

    ri = lax.broadcasted_iota(jnp.int32, (ck, ck), 0)
    ci = lax.broadcasted_iota(jnp.int32, (ck, ck), 1)
    tri = (ci >= ri) if rev else (ci <= ri)
    tri_b = jnp.where(tri, 1.0, 0.0).astype(BF16)
    tri_tb = jnp.where((ri >= ci) if rev else (ri <= ci), 1.0, 0.0).astype(BF16)
    eye_b = jnp.where(ri == ci, 1.0, 0.0).astype(BF16)

    def terms(q):
        hi = q.astype(BF16)
        r1 = q - hi.astype(F32)
        mid = r1.astype(BF16)
        return [hi, mid, (r1 - mid.astype(F32)).astype(BF16)]

    nck = tb // ck
    side = lambda q: jnp.concatenate([q[c * ck:(c + 1) * ck] for c in range(nck)], axis=1)
    da_s, ldt_s, dt_s = side(da), side(ldt), side(dt)
    cs_all = jnp.dot(jnp.concatenate([tri_b] * 3, axis=1), jnp.concatenate(terms(da_s), axis=0),
                     preferred_element_type=F32)
    rt_all = lax.dot_general(jnp.concatenate(terms(da_s) + terms(-ldt_s), axis=0),
                             jnp.concatenate([tri_tb] * 3 + [eye_b] * 3, axis=0),
                             (((0,), (0,)), ((), ())), preferred_element_type=F32)
    end = 0 if rev else ck - 1
    ecs_all = jnp.exp(cs_all)
    wgt_all = jnp.exp(cs_all[end:end + 1] - cs_all) * dt_s
    cs2_all = cs_all * LOG2E
    rt2_all = rt_all * LOG2E
    eh = lax.broadcasted_iota(jnp.int32, (heads, d_ssd), 0)
    ec = lax.broadcasted_iota(jnp.int32, (heads, d_ssd), 1)
    expand = jnp.where(ec // SSD_HEADDIM == eh, 1.0, 0.0).astype(BF16)
    expand2 = jnp.concatenate([expand, expand], axis=0)
    lane = lax.broadcasted_iota(jnp.int32, (ck, LANE), 1)

    def widen(q):
        hi = q.astype(BF16)
        lo = (q - hi.astype(F32)).astype(BF16)
        return jnp.dot(jnp.concatenate([hi, lo], axis=1), expand2, preferred_element_type=F32)

    order = range(tb // ck - 1, -1, -1) if rev else range(tb // ck)
    for c in order:
        r0 = c * ck
        x_bf = xs_ref[0, r0:r0 + ck, :d_ssd]
        b_bf = xs_ref[0, r0:r0 + ck, d_ssd:d_ssd + gn]
        c_bf = xs_ref[0, r0:r0 + ck, d_ssd + gn:]
        hs = slice(c * heads, (c + 1) * heads)
        ecs_w = widen(ecs_all[:, hs])
        xw = (x_bf.astype(F32) * widen(wgt_all[:, hs])).astype(BF16)
        cs2 = cs2_all[:, hs]
        r_t2 = rt2_all[hs, :]
        for g in range(SSD_GROUPS):
            bg = b_bf[:, g * SSD_STATE:(g + 1) * SSD_STATE]
            cg = c_bf[:, g * SSD_STATE:(g + 1) * SSD_STATE]
            cb = lax.dot_general(cg, bg, (((1,), (1,)), ((), ())), preferred_element_type=F32)
            s_in = st_ref[g]
            y_off = jnp.dot(cg, s_in.astype(BF16), preferred_element_type=F32) * ecs_w[:, g * gw:(g + 1) * gw]
            new = lax.dot_general(bg, xw[:, g * gw:(g + 1) * gw], (((0,), (0,)), ((), ())),
                                  preferred_element_type=F32)
            st_ref[g] = ecs_w[end:end + 1, g * gw:(g + 1) * gw] * s_in + new
            for p in range(gw // LANE):
                ha = (g * gw + p * LANE) // SSD_HEADDIM
                lm = []
                for h in (ha, ha + 1):
                    seg2 = cs2[:, h:h + 1] - r_t2[h:h + 1, :]
                    lm.append(jnp.where(tri, jnp.exp2(seg2), 0.0) * cb)
                lhs = jnp.concatenate(lm, axis=1).astype(BF16)
                xpair = x_bf[:, g * gw + p * LANE:g * gw + (p + 1) * LANE]
                zero = jnp.zeros_like(xpair)
                rhs = jnp.concatenate([jnp.where(lane < SSD_HEADDIM, xpair, zero),
                                       jnp.where(lane >= SSD_HEADDIM, xpair, zero)], axis=0)
                col = g * gw + p * LANE
                yb_ref[r0:r0 + ck, col:col + LANE] = (
                    jnp.dot(lhs, rhs, preferred_element_type=F32) + y_off[:, p * LANE:(p + 1) * LANE]
                ).astype(yb_ref.dtype)

    if rev:
        z = z_ref[0].astype(F32)
        x_all = xs_ref[0, :, :d_ssd].astype(F32)
        tot_y = (yb_ref[...] + yf_ref[0].astype(F32) + dsk_ref[...] * x_all) * (z * jax.nn.sigmoid(z))
        y_ref[0] = (_rms(tot_y) * nw_ref[...]).astype(BF16)

    @pl.when(i == nblk - 1)
    def _():
        sfin_ref[0] = st_ref[...]


def _ssd_dir(rev, xs, dt_raw, dt_bias, a_log, s0, tb, extra=None):
    bsz, s, dxbc = xs.shape
    heads = dt_raw.shape[-1] // 2
    d_ssd = heads * SSD_HEADDIM
    gw = d_ssd // SSD_GROUPS
    nblk = s // tb
    pos = (lambda i: nblk - 1 - i) if rev else (lambda i: i)
    in_specs = [
        pl.BlockSpec((1, tb, dxbc), lambda b, i: (b, pos(i), 0)),
        pl.BlockSpec((1, tb, 2 * heads), lambda b, i: (b, pos(i), 0)),
        _resident((1, 2 * heads)),
        _resident((1, 2 * heads)),
        pl.BlockSpec((1, SSD_GROUPS, SSD_STATE, gw), lambda b, i: (b, 0, 0, 0)),
    ]
    args = [xs, dt_raw, dt_bias.reshape(1, -1), a_log.reshape(1, -1), s0]
    st_shape = jax.ShapeDtypeStruct((bsz, SSD_GROUPS, SSD_STATE, gw), F32)
    st_spec = pl.BlockSpec((1, SSD_GROUPS, SSD_STATE, gw), lambda b, i: (b, 0, 0, 0))
    scratch = [pltpu.VMEM((SSD_GROUPS, SSD_STATE, gw), F32)]
    if rev:
        yf, z, d_skip, norm_w = extra
        in_specs += [
            pl.BlockSpec((1, tb, d_ssd), lambda b, i: (b, pos(i), 0)),
            pl.BlockSpec((1, tb, d_ssd), lambda b, i: (b, pos(i), 0)),
            _resident((1, d_ssd)),
            _resident((1, d_ssd)),
        ]
        args += [yf, z, jnp.repeat(d_skip, SSD_HEADDIM)[None], norm_w[None]]
        y_dtype = BF16
        scratch.append(pltpu.VMEM((tb, d_ssd), F32))
    else:
        y_dtype = BF16
    return pl.pallas_call(
        functools.partial(_ssd_kernel, rev, nblk, tb, heads),
        grid=(bsz, nblk),
        in_specs=in_specs,
        out_specs=[pl.BlockSpec((1, tb, d_ssd), lambda b, i: (b, pos(i), 0)), st_spec],
        out_shape=[jax.ShapeDtypeStruct((bsz, s, d_ssd), y_dtype), st_shape],
        scratch_shapes=scratch,
        compiler_params=_cparams("parallel", "arbitrary"),
        name="ssd_bwd" if rev else "ssd_fwd",
    )(*args)


def _ssd_bidir(z, xs, dt_raw, p, s0_f, s0_b, tb):
    yf, s_f = _ssd_dir(False, xs, dt_raw, p["dt_bias"], p["a_log"], s0_f, tb)
    y, s_b = _ssd_dir(True, xs, dt_raw, p["dt_bias"], p["a_log"], s0_b, tb,
                      extra=(yf, z, p["d_skip"], p["norm_w"]))
    return y, s_f, s_b


def _shift_rows(x, s, fill, up):
    c = x.shape[0]
    rolled = pltpu.roll(x, (c - s) if up else s, 0)
    row = lax.broadcasted_iota(jnp.int32, x.shape, 0)
    keep = (row < c - s) if up else (row >= s)
    return jnp.where(keep, rolled, fill)


def _lru_kernel(nr, nc, rb, continuous, lx_ref, lg_ref, cw_ref, cb_ref, wg_ref, bg_ref, spl_ref, h0_ref,
                g_ref, hfin_ref, xp_ref, a_ref, b_ref):
    cwid = lx_ref.shape[-1]
    xin = lx_ref[0].astype(F32)
    xp_ref[2:nr + 2] = xin
    if continuous:
        for k in range(2):
            xp_ref[k] = _shift_rows(xin[nr - 2 + k], 1, 0.0, up=False)
        xp_ref[nr + 2] = _shift_rows(xin[0], 1, 0.0, up=True)
    else:
        xp_ref[0:2] = jnp.zeros((2, nc, cwid), F32)
        xp_ref[nr + 2:nr + 3] = jnp.zeros((1, nc, cwid), F32)

    def gates(t, _):
        r0 = pl.multiple_of(t * rb, rb)
        xc = cb_ref[...] + cw_ref[0:1] * xp_ref[pl.ds(r0, rb)]
        for k in range(1, CONV_K):
            xc = xc + cw_ref[k:k + 1] * xp_ref[pl.ds(r0 + k, rb)]
        xc2 = xc.reshape(rb * nc, cwid)
        gt = jnp.dot(xc2.astype(BF16), wg_ref[0], preferred_element_type=F32) + bg_ref[0]
        x_half = 0.5 * xc2
        for d in range(2):
            t_r = jnp.tanh(gt[:, (2 * d) * cwid:(2 * d + 1) * cwid])
            t_i = jnp.tanh(gt[:, (2 * d + 1) * cwid:(2 * d + 2) * cwid])
            c_half = spl_ref[0, d:d + 1]
            nla = c_half * t_r + c_half
            a = jnp.exp2(nla * (-LOG2E))
            one_m_a2 = jnp.tanh(nla) * (a * a + 1.0)
            root = one_m_a2 * lax.rsqrt(jnp.maximum(one_m_a2, F32_TINY))
            a_ref[d, pl.ds(r0, rb)] = a.reshape(rb, nc, cwid)
            b_ref[d, pl.ds(r0, rb)] = (root * (t_i + 1.0) * x_half).reshape(rb, nc, cwid)
        return 0

    lax.fori_loop(0, nr // rb, gates, 0)

    hin = []
    for d in range(2):
        def totals(t, carry):
            h, p = carry
            r = (nr - 1 - t) if d else t
            a = a_ref[d, r]
            return a * h + b_ref[d, r], a * p

        h_l, p_l = lax.fori_loop(0, nr, totals, (jnp.zeros((nc, cwid), F32), jnp.ones((nc, cwid), F32)),
                                 unroll=4)
        sa, sb = p_l, h_l
        s = 1
        while s < nc:
            sb = sa * _shift_rows(sb, s, 0.0, up=bool(d)) + sb
            sa = sa * _shift_rows(sa, s, 1.0, up=bool(d))
            s *= 2
        h0 = h0_ref[0, d:d + 1]
        after = sa * h0 + sb
        hin.append(_shift_rows(after, 1, h0, up=bool(d)) if nc > 1 else jnp.broadcast_to(h0, (nc, cwid)))
        last = 0 if d else nc - 1
        hfin_ref[0, d:d + 1] = after[last:last + 1]

    def fwd(r, h):
        h = a_ref[0, r] * h + b_ref[0, r]
        b_ref[0, r] = h
        return h

    lax.fori_loop(0, nr, fwd, hin[0], unroll=4)

    def bwd(t, h):
        r = nr - 1 - t
        h = a_ref[1, r] * h + b_ref[1, r]
        v = lg_ref[0, r].astype(F32)
        t = jnp.tanh(v * (GELU_K + (GELU_K * 0.044715) * (v * v)))
        sv = (h + b_ref[0, r]) * (0.5 * v)
        g_ref[0, r] = (sv + sv * t).astype(BF16)
        return h

    lax.fori_loop(0, nr, bwd, hin[1], unroll=4)


def _lru(lx, lg, p, h0, rb, continuous):
    bsz, nr, nc, d = lx.shape
    cwid = p["spl"].shape[-1]
    ng = d // cwid
    return pl.pallas_call(
        functools.partial(_lru_kernel, nr, nc, rb, continuous),
        grid=(bsz, ng),
        in_specs=[
            pl.BlockSpec((1, nr, nc, cwid), lambda b, j: (b, 0, 0, j)),
            pl.BlockSpec((1, nr, nc, cwid), lambda b, j: (b, 0, 0, j)),
            pl.BlockSpec((CONV_K, cwid), lambda b, j: (0, j)),
            pl.BlockSpec((1, cwid), lambda b, j: (0, j)),
            pl.BlockSpec((1, cwid, 4 * cwid), lambda b, j: (j, 0, 0)),
            pl.BlockSpec((1, 1, 4 * cwid), lambda b, j: (j, 0, 0)),
            pl.BlockSpec((1, 2, cwid), lambda b, j: (j, 0, 0)),
            pl.BlockSpec((1, 2, cwid), lambda b, j: (b, 0, j)),
        ],
        out_specs=[
            pl.BlockSpec((1, nr, nc, cwid), lambda b, j: (b, 0, 0, j)),
            pl.BlockSpec((1, 2, cwid), lambda b, j: (b, 0, j)),
        ],
        out_shape=[jax.ShapeDtypeStruct((bsz, nr, nc, d), BF16), jax.ShapeDtypeStruct((bsz, 2, d), F32)],
        scratch_shapes=[
            pltpu.VMEM((nr + 3, nc, cwid), F32),
            pltpu.VMEM((2, nr, nc, cwid), F32),
            pltpu.VMEM((2, nr, nc, cwid), F32),
        ],
        compiler_params=_cparams("parallel", "arbitrary"),
        name="rglru",
    )(lx, lg, p["conv_w"], p["conv_b"][None], p["wg"], p["bg"], p["spl"], h0)


def _lru_params(conv_w, conv_b, rw, rb_, iw, ib, lam, cwid):
    d = conv_w.shape[-1]
    ng = d // cwid
    nb = cwid // LRU_BW

    def block_diag(w):
        w = w.reshape(ng, nb, LRU_BW, LRU_BW)
        eye = jnp.eye(nb, dtype=w.dtype)
        return jnp.einsum("gakj,ab->gakbj", w, eye).reshape(ng, cwid, cwid)

    wg = jnp.concatenate([block_diag(rw[0]), block_diag(iw[0]), block_diag(rw[1]), block_diag(iw[1])], axis=-1)
    bg = 0.5 * jnp.concatenate([rb_[0].reshape(ng, cwid), ib[0].reshape(ng, cwid),
                                rb_[1].reshape(ng, cwid), ib[1].reshape(ng, cwid)], axis=-1)[:, None, :]
    spl = (0.5 * RGLRU_C) * jax.nn.softplus(-lam.astype(F32)).reshape(2, ng, cwid).transpose(1, 0, 2)
    return dict(conv_w=conv_w, conv_b=conv_b, wg=(0.5 * wg).astype(BF16), bg=bg, spl=spl)


def _swiglu_acc(h, w1_ref, w3_ref, w2_ref, nf):
    f = w1_ref.shape[-1]
    step = -(-f // (nf * MXU_DEPTH)) * MXU_DEPTH
    edges = list(range(0, f, step)) + [f]
    acc = None
    for lo, hi in zip(edges[:-1], edges[1:]):
        a = jnp.dot(h, w1_ref[:, lo:hi], preferred_element_type=F32)
        b = jnp.dot(h, w3_ref[:, lo:hi], preferred_element_type=F32)
        pj = (a * jax.nn.sigmoid(a) * b).astype(BF16)
        o = jnp.dot(pj, w2_ref[lo:hi, :], preferred_element_type=F32)
        acc = o if acc is None else acc + o
    return acc


def _outproj_kernel(route, final, y_ref, g_ref, x_ref, mod_ref, lnw_ref, wo_ref, n2w_ref, *rest):
    if route:
        rw_ref, x1_ref, h2_ref, rt_ref = rest
    elif final:
        w1_ref, w3_ref, w2_ref, fw_ref, o_ref = rest
    else:
        w1_ref, w3_ref, w2_ref, o_ref = rest
    m = mod_ref[0]
    gl = (_rms(g_ref[0].astype(F32)) * lnw_ref[...]).astype(BF16)
    dy = y_ref.shape[-1]
    proj = (jnp.dot(y_ref[0], wo_ref[0:dy], preferred_element_type=F32)
            + jnp.dot(gl, wo_ref[dy:], preferred_element_type=F32))
    x1 = x_ref[0] + m[2:3] * proj
    h2 = _rms(x1) * n2w_ref[...] * (1.0 + m[4:5]) + m[3:4]
    if not route:
        out = x1 + m[5:6] * _swiglu_acc(h2.astype(BF16), w1_ref, w3_ref, w2_ref, 2)
        if final:
            out = _rms(out) * fw_ref[...]
        o_ref[0] = out
        return
    x1_ref[0] = x1
    h2_ref[0] = _pack_pairs(h2)
    h_hi = h2.astype(BF16)
    h_lo = (h2 - h_hi.astype(F32)).astype(BF16)
    r_hi = jnp.dot(h_hi, rw_ref[...], preferred_element_type=F32)
    logits = (r_hi[:, :LANE] + r_hi[:, LANE:]
              + jnp.dot(h_lo, rw_ref[:, :LANE], preferred_element_type=F32))
    lane = lax.broadcasted_iota(jnp.int32, logits.shape, 1).astype(F32)
    neg = -jnp.inf
    logits = jnp.where(lane < N_EXPERTS, logits, neg)
    m1 = jnp.max(logits, axis=-1, keepdims=True)
    i1 = jnp.min(jnp.where(logits == m1, lane, float(LANE)), axis=-1, keepdims=True)
    rest_l = jnp.where(lane == i1, neg, logits)
    m2 = jnp.max(rest_l, axis=-1, keepdims=True)
    i2 = jnp.min(jnp.where(rest_l == m2, lane, float(LANE)), axis=-1, keepdims=True)
    e = jnp.exp(m2 - m1)
    p1 = 1.0 / (1.0 + e)
    p2 = e / (1.0 + e)
    rt_ref[0] = jnp.where(lane == 0.0, i1, jnp.where(lane == 1.0, i2,
                          jnp.where(lane == 2.0, p1, jnp.where(lane == 3.0, p2, 0.0))))


def _outproj(y, g, x, mod_l, mod_row, lru_norm_w, w_out, norm2_w, tm, router_w=None, ffn_w=None, final_w=None):
    bsz, s, d = x.shape
    route = router_w is not None
    final = final_w is not None and not route
    in_specs = [
        pl.BlockSpec((1, tm, y.shape[-1]), lambda b, i: (b, i, 0)),
        pl.BlockSpec((1, tm, g.shape[-1]), lambda b, i: (b, i, 0)),
        pl.BlockSpec((1, tm, d), lambda b, i: (b, i, 0)),
        pl.BlockSpec((1, N_MOD, d), lambda b, i: (mod_row(b), 0, 0)),
        _resident((1, g.shape[-1])),
        _resident(w_out.shape),
        _resident((1, d)),
    ]
    args = [y, g, x, mod_l, lru_norm_w[None], w_out, norm2_w[None]]
    tok = pl.BlockSpec((1, tm, d), lambda b, i: (b, i, 0))
    tok_shape = jax.ShapeDtypeStruct((bsz, s, d), F32)
    if route:
        rw = jnp.zeros((d, LANE), F32).at[:, :N_EXPERTS].set(router_w)
        rw_hi = rw.astype(BF16)
        in_specs.append(_resident((d, 2 * LANE)))
        args.append(jnp.concatenate([rw_hi, (rw - rw_hi.astype(F32)).astype(BF16)], axis=1))
        out_specs = [tok, pl.BlockSpec((1, tm, d // 2), lambda b, i: (b, i, 0)),
                     pl.BlockSpec((1, tm, LANE), lambda b, i: (b, i, 0))]
        out_shape = [tok_shape, jax.ShapeDtypeStruct((bsz, s, d // 2), jnp.uint32),
                     jax.ShapeDtypeStruct((bsz, s, LANE), F32)]
    else:
        in_specs += [_resident(w.shape) for w in ffn_w]
        args += list(ffn_w)
        if final:
            in_specs.append(_resident((1, d)))
            args.append(final_w[None])
        out_specs, out_shape = tok, tok_shape
    return pl.pallas_call(
        functools.partial(_outproj_kernel, route, final),
        grid=(bsz, s // tm),
        in_specs=in_specs,
        out_specs=out_specs,
        out_shape=out_shape,
        compiler_params=_cparams("parallel", "arbitrary"),
        name="outproj_route" if route else "outproj_ffn",
    )(*args)


def _sc_worker_id():
    return lax.axis_index("subcore") * SC_CORES + lax.axis_index("core")


def _sc_mesh():
    return plsc.VectorSubcoreMesh(core_axis_name="core", subcore_axis_name="subcore")


def _sc_scatter2(x, i0, i1, n_rows):
    m, d = x.shape
    per_w = m // SC_WORKERS
    assert per_w % SC_WINDOW == 0

    @functools.partial(
        pl.kernel, out_type=jax.ShapeDtypeStruct((n_rows, d), x.dtype), mesh=_sc_mesh(),
        scratch_types=[pltpu.VMEM((SC_WINDOW,), jnp.int32), pltpu.VMEM((SC_WINDOW, d), x.dtype),
                       pltpu.SemaphoreType.DMA])
    def scatter(x_hbm, i0_hbm, i1_hbm, o_hbm, idx_v, rows_v, sem):
        wid = _sc_worker_id()

        @pl.loop(0, per_w // SC_WINDOW)
        def _(j):
            base = wid * per_w + j * SC_WINDOW
            pltpu.sync_copy(x_hbm.at[pl.ds(base, SC_WINDOW)], rows_v)
            for i_hbm in (i0_hbm, i1_hbm):
                pltpu.sync_copy(i_hbm.at[pl.ds(base, SC_WINDOW)], idx_v)
                pltpu.async_copy(rows_v, o_hbm.at[idx_v], sem).wait()

    return scatter(x, i0, i1)


def _sc_gather2(table, i0, i1):
    m = i0.shape[0]
    d = table.shape[1]
    per_w = m // SC_WORKERS
    assert per_w % SC_WINDOW == 0
    out = jax.ShapeDtypeStruct((m, d), table.dtype)

    @functools.partial(
        pl.kernel, out_type=(out, out), mesh=_sc_mesh(),
        scratch_types=[pltpu.VMEM((SC_WINDOW,), jnp.int32), pltpu.VMEM((SC_WINDOW, d), table.dtype),
                       pltpu.SemaphoreType.DMA])
    def gather(t_hbm, i0_hbm, i1_hbm, a_hbm, b_hbm, idx_v, rows_v, sem):
        wid = _sc_worker_id()

        @pl.loop(0, per_w // SC_WINDOW)
        def _(j):
            base = wid * per_w + j * SC_WINDOW
            for i_hbm, o_hbm in ((i0_hbm, a_hbm), (i1_hbm, b_hbm)):
                pltpu.sync_copy(i_hbm.at[pl.ds(base, SC_WINDOW)], idx_v)
                pltpu.async_copy(t_hbm.at[idx_v], rows_v, sem).wait()
                pltpu.sync_copy(rows_v, o_hbm.at[pl.ds(base, SC_WINDOW)])

    return gather(table, i0, i1)


def _experts_kernel(nf, te_ref, tv_ref, h_ref, w1_ref, w3_ref, w2_ref, o_ref):
    valid = tv_ref[pl.program_id(0)]

    @pl.when(valid > 0)
    def _():
        row = lax.broadcasted_iota(jnp.int32, h_ref.shape, 0)
        words = jnp.where(row < valid, h_ref[...], jnp.uint32(0))
        lo, hi = _unpack_pairs(words)
        h = jnp.concatenate([lo, hi], axis=1).astype(BF16)
        o_ref[...] = _pack_pairs(_swiglu_acc(h, w1_ref.at[0], w3_ref.at[0], w2_ref.at[0], nf))

    @pl.when(valid <= 0)
    def _():
        o_ref[...] = jnp.zeros_like(o_ref)


def _experts(h_sorted, tile_expert, tile_valid, w1, w3, w2, tile):
    n, dw = h_sorted.shape
    d, f = w1.shape[1:]
    grid_spec = pltpu.PrefetchScalarGridSpec(
        num_scalar_prefetch=2,
        grid=(n // tile,),
        in_specs=[
            pl.BlockSpec((tile, dw), lambda i, te, tv: (i, 0)),
            pl.BlockSpec((1, d, f), lambda i, te, tv: (te[i], 0, 0), pipeline_mode=pl.Buffered(1)),
            pl.BlockSpec((1, d, f), lambda i, te, tv: (te[i], 0, 0), pipeline_mode=pl.Buffered(1)),
            pl.BlockSpec((1, f, d), lambda i, te, tv: (te[i], 0, 0), pipeline_mode=pl.Buffered(1)),
        ],
        out_specs=pl.BlockSpec((tile, dw), lambda i, te, tv: (i, 0)),
    )
    return pl.pallas_call(
        functools.partial(_experts_kernel, 2),
        grid_spec=grid_spec,
        out_shape=jax.ShapeDtypeStruct((n, dw), jnp.uint32),
        compiler_params=_cparams("arbitrary"),
        name="experts",
    )(tile_expert, tile_valid, h_sorted, w1, w3, w2)


def _combine_kernel(x1_ref, ya_ref, yb_ref, rt_ref, mod_ref, fw_ref, o_ref):
    m = mod_ref[0]
    rt = rt_ref[0]
    a_lo, a_hi = _unpack_pairs(ya_ref[0])
    b_lo, b_hi = _unpack_pairs(yb_ref[0])
    p1, p2 = rt[:, 2:3], rt[:, 3:4]
    moe = jnp.concatenate([p1 * a_lo + p2 * b_lo, p1 * a_hi + p2 * b_hi], axis=1)
    o_ref[0] = _rms(x1_ref[0] + m[5:6] * moe) * fw_ref[...]


def _moe(x1, h2, route, mod_l, mod_row, w1, w3, w2, final_w, tm, tile):
    bsz, s, d = x1.shape
    n = bsz * s
    ex = route[..., :2].astype(jnp.int32).reshape(2 * n)
    onehot = (ex[:, None] == jnp.arange(N_EXPERTS, dtype=jnp.int32)[None, :]).astype(jnp.int32)
    csum = jnp.cumsum(onehot, axis=0)
    cnt = csum[-1]
    rank = jnp.sum((csum - onehot) * onehot, axis=1)
    padded = ((cnt + tile - 1) // tile) * tile
    ends = jnp.cumsum(padded)
    pos = (ends - padded)[ex] + rank
    n_rows = 2 * n + N_EXPERTS * tile
    tile_start = jnp.arange(n_rows // tile, dtype=jnp.int32) * tile
    tile_expert = jnp.minimum(jnp.searchsorted(ends, tile_start, side="right"), N_EXPERTS - 1).astype(jnp.int32)
    group_end = (ends - padded + cnt)[tile_expert]
    tile_valid = jnp.clip(group_end - tile_start, 0, tile).astype(jnp.int32)

    pos2 = pos.reshape(n, 2)
    dw = h2.shape[-1]
    h_sorted = _sc_scatter2(h2.reshape(n, dw), pos2[:, 0], pos2[:, 1], n_rows)
    y_sorted = _experts(h_sorted, tile_expert, tile_valid, w1, w3, w2, tile)
    ya, yb = _sc_gather2(y_sorted, pos2[:, 0], pos2[:, 1])
    ya = ya.reshape(bsz, s, dw)
    yb = yb.reshape(bsz, s, dw)
    tok = pl.BlockSpec((1, tm, d), lambda b, i: (b, i, 0))
    words = pl.BlockSpec((1, tm, dw), lambda b, i: (b, i, 0))
    return pl.pallas_call(
        _combine_kernel,
        grid=(bsz, s // tm),
        in_specs=[tok, words, words,
                  pl.BlockSpec((1, tm, LANE), lambda b, i: (b, i, 0)),
                  pl.BlockSpec((1, N_MOD, d), lambda b, i: (mod_row(b), 0, 0)),
                  _resident((1, d))],
        out_specs=tok,
        out_shape=jax.ShapeDtypeStruct((bsz, s, d), F32),
        compiler_params=_cparams("parallel", "arbitrary"),
        name="moe_combine",
    )(x1, ya, yb, route, mod_l, final_w[None])


def _pick_tile(s, pref):
    t = min(s, pref)
    while s % t:
        t //= 2
    return t


def kernel(x, c, ctx, c_ctx, mod_w, mod_b, norm1_w, norm2_w, w_in, ssd_conv_w, ssd_conv_b, ssd_dt_bias,
           ssd_a_log, ssd_d, ssd_norm_w, lru_conv_w, lru_conv_b, lru_rw, lru_rb, lru_iw, lru_ib, lru_lambda,
           lru_norm_w, w_out, ffn_w1, ffn_w3, ffn_w2, router_w, moe_w1, moe_w3, moe_w2, final_norm_w):
    depth = w_in.shape[0]
    bsz, seq, d = x.shape
    lc = ctx.shape[1]
    rows = seq // GRID_W
    heads = ssd_dt_bias.shape[-1]
    d_ssd = heads * SSD_HEADDIM
    d_xbc = ssd_conv_w.shape[-1]
    d_lru = lru_conv_w.shape[-1]
    splits = (d_ssd, d_xbc, d_lru, d_lru)
    off_dt = d_ssd + d_xbc
    off_lx = off_dt + 2 * heads
    perm = jnp.concatenate([jnp.arange(0, off_dt), jnp.arange(off_lx, off_lx + 2 * d_lru),
                            jnp.arange(off_dt, off_lx)])
    ctx_cols = 8
    ctx_rows = lc // ctx_cols
    lru_cw = 256

    mod = _modulation(c, c_ctx, mod_w, mod_b)
    lat_row = lambda b: b
    ctx_row = lambda b: bsz
    tm_lat = _pick_tile(seq, 512)
    tm_ctx = _pick_tile(lc, 512)
    tb_lat = _pick_tile(seq, 512)
    tb_ctx = _pick_tile(lc, 512)
    gw = d_ssd // SSD_GROUPS

    xc = ctx
    for l in range(depth):
        last = l == depth - 1
        mod_l = mod[l]
        w_perm = jnp.pad(w_in[l][:, perm], ((0, 0), (0, LANE - 2 * heads))).astype(BF16)
        ssd_p = dict(dt_bias=ssd_dt_bias[l], a_log=ssd_a_log[l], d_skip=ssd_d[l], norm_w=ssd_norm_w[l])
        lru_p = _lru_params(lru_conv_w[l], lru_conv_b[l], lru_rw[l], lru_rb[l], lru_iw[l], lru_ib[l],
                            lru_lambda[l], lru_cw)
        wo = w_out[l].astype(BF16)

        zc, xbcc, lxc, lgc, dtc = _inproj(xc, mod_l, ctx_row, norm1_w[l], w_perm, ssd_conv_w[l], ssd_conv_b[l],
                                          splits, 2 * heads, tm_ctx)
        zl, xbcl, lxl, lgl, dtl = _inproj(x, mod_l, lat_row, norm1_w[l], w_perm, ssd_conv_w[l], ssd_conv_b[l],
                                          splits, 2 * heads, tm_lat)

        zero_s = jnp.zeros((bsz, SSD_GROUPS, SSD_STATE, gw), F32)
        yc_ssd, s_f, s_b = _ssd_bidir(zc, xbcc, dtc, ssd_p, zero_s, zero_s, tb_ctx)
        yl_ssd, _, _ = _ssd_bidir(zl, xbcl, dtl, ssd_p, s_f, s_b, tb_lat)

        to_grid = lambda v: v.reshape(bsz, ctx_cols, ctx_rows, d_lru).transpose(0, 2, 1, 3)
        gc, h_fin = _lru(to_grid(lxc), to_grid(lgc), lru_p, jnp.zeros((bsz, 2, d_lru), F32), ctx_rows, True)
        gl, _ = _lru(lxl.reshape(bsz, rows, GRID_W, d_lru), lgl.reshape(bsz, rows, GRID_W, d_lru), lru_p,
                     h_fin, 8, False)
        gl = gl.reshape(bsz, seq, d_lru)

        if l % 2 == 0:
            w1, w3, w2 = (ffn_w1[l // 2].astype(BF16), ffn_w3[l // 2].astype(BF16), ffn_w2[l // 2].astype(BF16))
            x_next = _outproj(yl_ssd, gl, x, mod_l, lat_row, lru_norm_w[l], wo, norm2_w[l], tm_lat,
                              ffn_w=(w1, w3, w2), final_w=final_norm_w if last else None)
            if not last:
                gc = gc.transpose(0, 2, 1, 3).reshape(bsz, lc, d_lru)
                xc = _outproj(yc_ssd, gc, xc, mod_l, ctx_row, lru_norm_w[l], wo, norm2_w[l], tm_ctx,
                              ffn_w=(w1, w3, w2))
            x = x_next
        else:
            w1, w3, w2 = (moe_w1[l // 2].astype(BF16), moe_w3[l // 2].astype(BF16), moe_w2[l // 2].astype(BF16))
            assert last, "a routed layer that is not the last layer is not implemented"
            x1, h2, route = _outproj(yl_ssd, gl, x, mod_l, lat_row, lru_norm_w[l], wo, norm2_w[l], tm_lat,
                                     router_w=router_w[l // 2])
            x = _moe(x1, h2, route, mod_l, lat_row, w1, w3, w2, final_norm_w, tm_lat, 512)
    return x
```

```python
import functools

import jax
import jax.numpy as jnp
from jax import lax
from jax.experimental import pallas as pl
from jax.experimental.pallas import tpu as pltpu
from jax.experimental.pallas import tpu_sc as plsc

F32 = jnp.float32
BF16 = jnp.bfloat16
HI = lax.Precision.HIGHEST

EPS = 1e-6
N_MOD = 6
GRID_W = 64
SSD_HEADDIM = 64
SSD_GROUPS = 2
SSD_STATE = 128
SSD_CHUNK = 128
CONV_K = 4
LRU_BW = 64
RGLRU_C = 8.0
N_EXPERTS = 8
LOG2E = 1.4426950408889634
GELU_K = 0.7978845608028654
F32_TINY = float(jnp.finfo(jnp.float32).tiny)
HALO = 16
LANE = 128
MXU_DEPTH = 256
MOD_ROWS = 24
VMEM_LIMIT = 56 * 1024 * 1024
SC_CORES = 2
SC_WORKERS = 32
SC_WINDOW = 128


def _cparams(*sem):
    return pltpu.CompilerParams(dimension_semantics=sem, vmem_limit_bytes=VMEM_LIMIT)


def _resident(shape):
    nd = len(shape)
    return pl.BlockSpec(shape, lambda *_: (0,) * nd, pipeline_mode=pl.Buffered(1))


def _rms(x):
    return x * lax.rsqrt(jnp.mean(x * x, axis=-1, keepdims=True) + EPS)


def _pack_pairs(v):
    k = v.shape[1] // 2
    lo = lax.bitcast_convert_type(v[:, :k].astype(BF16).astype(F32), jnp.uint32)
    hi = lax.bitcast_convert_type(v[:, k:].astype(BF16).astype(F32), jnp.uint32)
    return hi | (lo >> 16)


def _unpack_pairs(w):
    lo = lax.bitcast_convert_type(w << 16, F32)
    hi = lax.bitcast_convert_type(w & jnp.uint32(0xFFFF0000), F32)
    return lo, hi


def _mod_kernel(s_ref, w_ref, b_ref, o_ref):
    s = s_ref[...]
    s = s * jax.nn.sigmoid(s)
    o_ref[0] = jnp.dot(s, w_ref[0], precision=HI, preferred_element_type=F32) + b_ref[0]


def _modulation(c, c_ctx, mod_w, mod_b):
    depth, d, n = mod_w.shape
    bsz = c.shape[0]
    assert bsz + 1 <= MOD_ROWS
    s = jnp.concatenate([c, c_ctx[None], jnp.zeros((MOD_ROWS - bsz - 1, d), F32)], axis=0)
    tn = n // 4
    out = pl.pallas_call(
        _mod_kernel,
        grid=(depth, n // tn),
        in_specs=[
            pl.BlockSpec((MOD_ROWS, d), lambda l, j: (0, 0)),
            pl.BlockSpec((1, d, tn), lambda l, j: (l, 0, j)),
            pl.BlockSpec((1, 1, tn), lambda l, j: (l, 0, j)),
        ],
        out_specs=pl.BlockSpec((1, MOD_ROWS, tn), lambda l, j: (l, 0, j)),
        out_shape=jax.ShapeDtypeStruct((depth, MOD_ROWS, n), F32),
        compiler_params=_cparams("arbitrary", "arbitrary"),
        name="modulation",
    )(s, mod_w, mod_b[:, None, :])
    return out.reshape(depth, MOD_ROWS, N_MOD, d)


def _inproj_kernel(splits, dt_w, nt, x_ref, xprev_ref, xnext_ref, mod_ref, nw_ref, w_ref, cw_ref, cb_ref,
                   z_ref, xs_ref, lx_ref, lg_ref, dt_ref):
    i = pl.program_id(1)
    m = mod_ref[0]
    tm = x_ref.shape[1]

    def norm_mod(v):
        return (_rms(v) * nw_ref[...] * (1.0 + m[1:2]) + m[0:1]).astype(BF16)

    h = norm_mod(x_ref[0])

    def proj(a, b):
        return jnp.dot(h, w_ref[:, a:b], preferred_element_type=F32)

    o = [0]
    for s in splits:
        o.append(o[-1] + s)
    h_ext = jnp.concatenate([norm_mod(xprev_ref[0]), h, norm_mod(xnext_ref[0])], axis=0)
    xe = jnp.dot(h_ext, w_ref[:, o[1]:o[2]], preferred_element_type=F32)
    row = lax.broadcasted_iota(jnp.int32, (tm + 2 * HALO, 1), 0)
    outside = jnp.logical_or(jnp.logical_and(row < HALO, i == 0),
                             jnp.logical_and(row >= tm + HALO, i == nt - 1))
    xe = jnp.where(outside, 0.0, xe)
    n_ext = tm + 2 * HALO

    acc = cb_ref[...] + cw_ref[2:3] * xe[HALO:HALO + tm]
    for k in (0, 1, 3):
        acc = acc + cw_ref[k:k + 1] * pltpu.roll(xe, (2 - k) % n_ext, 0)[HALO:HALO + tm]
    xs_ref[0] = (acc * jax.nn.sigmoid(acc)).astype(BF16)

    z_ref[0] = proj(o[0], o[1]).astype(BF16)
    lx_ref[0] = proj(o[2], o[3]).astype(BF16)
    lg_dt = proj(o[3], o[4] + LANE)
    lg_ref[0] = lg_dt[:, :splits[3]].astype(BF16)
    dt_ref[0] = lg_dt[:, splits[3]:splits[3] + dt_w]


def _inproj(x, mod_l, mod_row, norm_w, w_perm, conv_w, conv_b, splits, dt_w, tm):
    bsz, s, d = x.shape
    n_all = w_perm.shape[1]
    nt = s // tm
    nh = tm // HALO
    outs =[jax.ShapeDtypeStruct((bsz, s, w), BF16) for w in splits] + [jax.ShapeDtypeStruct((bsz, s, dt_w), F32)]
    return pl.pallas_call(
        functools.partial(_inproj_kernel, splits, dt_w, nt),
        grid=(bsz, nt),
        in_specs=[
            pl.BlockSpec((1, tm, d), lambda b, i: (b, i, 0)),
            pl.BlockSpec((1, HALO, d), lambda b, i: (b, jnp.maximum(i * nh - 1, 0), 0)),
            pl.BlockSpec((1, HALO, d), lambda b, i: (b, jnp.minimum((i + 1) * nh, s // HALO - 1), 0)),
            pl.BlockSpec((1, N_MOD, d), lambda b, i: (mod_row(b), 0, 0)),
            _resident((1, d)),
            _resident((d, n_all)),
            _resident((CONV_K, splits[1])),
            _resident((1, splits[1])),
        ],
        out_specs=[pl.BlockSpec((1, tm, w), lambda b, i: (b, i, 0)) for w in splits]
        + [pl.BlockSpec((1, tm, dt_w), lambda b, i: (b, i, 0))],
        out_shape=outs,
        compiler_params=_cparams("parallel", "arbitrary"),
        name="inproj",
    )(x, x, x, mod_l, norm_w[None], w_perm, conv_w, conv_b[None])


def _softplus(x):
    return jnp.maximum(x, 0.0) + jnp.log1p(jnp.exp(-jnp.abs(x)))


def _ssd_kernel(rev, nblk, tb, heads, *refs):
    if rev:
        (xs_ref, dt_ref, dtb_ref, alog_ref, s0_ref,
         yf_ref, z_ref, dsk_ref, nw_ref, y_ref, sfin_ref, st_ref, yb_ref) = refs
    else:
        (xs_ref, dt_ref, dtb_ref, alog_ref, s0_ref, y_ref, sfin_ref, st_ref) = refs
        yb_ref = y_ref.at[0]
    i = pl.program_id(1)
    d_ssd = heads * SSD_HEADDIM
    gn = SSD_GROUPS * SSD_STATE
    gw = d_ssd // SSD_GROUPS
    ck = SSD_CHUNK

    @pl.when(i == 0)
    def _():
        st_ref[...] = s0_ref[0]

    d0 = heads if rev else 0
    dt = _softplus(dt_ref[0] + dtb_ref[...])[:, d0:d0 + heads]
    a_neg = -jnp.exp(alog_ref[...])[:, d0:d0 + heads]
    da = dt * a_neg
    ldt = jnp.where(dt > 0.0, jnp.log(dt), -1e30)

    ri = lax.broadcasted_iota(jnp.int32, (ck, ck), 0)
    ci = lax.broadcasted_iota(jnp.int32, (ck, ck), 1)
    tri = (ci >= ri) if rev else (ci <= ri)
    tri_b = jnp.where(tri, 1.0, 0.0).astype(BF16)
    tri_tb = jnp.where((ri >= ci) if rev else (ri <= ci), 1.0, 0.0).astype(BF16)
    eye_b = jnp.where(ri == ci, 1.0, 0.0).astype(BF16)

    def terms(q):
        hi = q.astype(BF16)
        r1 = q - hi.astype(F32)
        mid = r1.astype(BF16)
        return [hi, mid, (r1 - mid.astype(F32)).astype(BF16)]

    nck = tb // ck
    side = lambda q: jnp.concatenate([q[c * ck:(c + 1) * ck] for c in range(nck)], axis=1)
    da_s, ldt_s, dt_s = side(da), side(ldt), side(dt)
    cs_all = jnp.dot(jnp.concatenate([tri_b] * 3, axis=1), jnp.concatenate(terms(da_s), axis=0),
                     preferred_element_type=F32)
    rt_all = lax.dot_general(jnp.concatenate(terms(da_s) + terms(-ldt_s), axis=0),
                             jnp.concatenate([tri_tb] * 3 + [eye_b] * 3, axis=0),
                             (((0,), (0,)), ((), ())), preferred_element_type=F32)
    end = 0 if rev else ck - 1
    ecs_all = jnp.exp(cs_all)
    wgt_all = jnp.exp(cs_all[end:end + 1] - cs_all) * dt_s
    cs2_all = cs_all * LOG2E
    rt2_all = rt_all * LOG2E
    eh = lax.broadcasted_iota(jnp.int32, (heads, d_ssd), 0)
    ec = lax.broadcasted_iota(jnp.int32, (heads, d_ssd), 1)
    expand = jnp.where(ec // SSD_HEADDIM == eh, 1.0, 0.0).astype(BF16)
    expand2 = jnp.concatenate([expand, expand], axis=0)
    lane = lax.broadcasted_iota(jnp.int32, (ck, LANE), 1)

    def widen(q):
        hi = q.astype(BF16)
        lo = (q - hi.astype(F32)).astype(BF16)
        return jnp.dot(jnp.concatenate([hi, lo], axis=1), expand2, preferred_element_type=F32)

    order = range(tb // ck - 1, -1, -1) if rev else range(tb // ck)
    for c in order:
        r0 = c * ck
        x_bf = xs_ref[0, r0:r0 + ck, :d_ssd]
        b_bf = xs_ref[0, r0:r0 + ck, d_ssd:d_ssd + gn]
        c_bf = xs_ref[0, r0:r0 + ck, d_ssd + gn:]
        hs = slice(c * heads, (c + 1) * heads)
        ecs_w = widen(ecs_all[:, hs])
        xw = (x_bf.astype(F32) * widen(wgt_all[:, hs])).astype(BF16)
        cs2 = cs2_all[:, hs]
        r_t2 = rt2_all[hs, :]
        for g in range(SSD_GROUPS):
            bg = b_bf[:, g * SSD_STATE:(g + 1) * SSD_STATE]
            cg = c_bf[:, g * SSD_STATE:(g + 1) * SSD_STATE]
            cb = lax.dot_general(cg, bg, (((1,), (1,)), ((), ())), preferred_element_type=F32)
            s_in = st_ref[g]
            y_off = jnp.dot(cg, s_in.astype(BF16), preferred_element_type=F32) * ecs_w[:, g * gw:(g + 1) * gw]
            new = lax.dot_general(bg, xw[:, g * gw:(g + 1) * gw], (((0,), (0,)), ((), ())),
                                  preferred_element_type=F32)
            st_ref[g] = ecs_w[end:end + 1, g * gw:(g + 1) * gw] * s_in + new
            for p in range(gw // LANE):
                ha = (g * gw + p * LANE) // SSD_HEADDIM
                lm = []
                for h in (ha, ha + 1):
                    seg2 = cs2[:, h:h + 1] - r_t2[h:h + 1, :]
                    lm.append(jnp.where(tri, jnp.exp2(seg2), 0.0) * cb)
                lhs = jnp.concatenate(lm, axis=1).astype(BF16)
                xpair = x_bf[:, g * gw + p * LANE:g * gw + (p + 1) * LANE]
                zero = jnp.zeros_like(xpair)
                rhs = jnp.concatenate([jnp.where(lane < SSD_HEADDIM, xpair, zero),
                                       jnp.where(lane >= SSD_HEADDIM, xpair, zero)], axis=0)
                col = g * gw + p * LANE
                yb_ref[r0:r0 + ck, col:col + LANE] = (
                    jnp.dot(lhs, rhs, preferred_element_type=F32) + y_off[:, p * LANE:(p + 1) * LANE]
                ).astype(yb_ref.dtype)

    if rev:
        z = z_ref[0].astype(F32)
        x_all = xs_ref[0, :, :d_ssd].astype(F32)
        tot_y = (yb_ref[...] + yf_ref[0].astype(F32) + dsk_ref[...] * x_all) * (z * jax.nn.sigmoid(z))
        y_ref[0] = (_rms(tot_y) * nw_ref[...]).astype(BF16)

    @pl.when(i == nblk - 1)
    def _():
        sfin_ref[0] = st_ref[...]


def _ssd_dir(rev, xs, dt_raw, dt_bias, a_log, s0, tb, extra=None):
    bsz, s, dxbc = xs.shape
    heads = dt_raw.shape[-1] // 2
    d_ssd = heads * SSD_HEADDIM
    gw = d_ssd // SSD_GROUPS
    nblk = s // tb
    pos = (lambda i: nblk - 1 - i) if rev else (lambda i: i)
    in_specs = [
        pl.BlockSpec((1, tb, dxbc), lambda b, i: (b, pos(i), 0)),
        pl.BlockSpec((1, tb, 2 * heads), lambda b, i: (b, pos(i), 0)),
        _resident((1, 2 * heads)),
        _resident((1, 2 * heads)),
        pl.BlockSpec((1, SSD_GROUPS, SSD_STATE, gw), lambda b, i: (b, 0, 0, 0)),
    ]
    args = [xs, dt_raw, dt_bias.reshape(1, -1), a_log.reshape(1, -1), s0]
    st_shape = jax.ShapeDtypeStruct((bsz, SSD_GROUPS, SSD_STATE, gw), F32)
    st_spec = pl.BlockSpec((1, SSD_GROUPS, SSD_STATE, gw), lambda b, i: (b, 0, 0, 0))
    scratch = [pltpu.VMEM((SSD_GROUPS, SSD_STATE, gw), F32)]
    if rev:
        yf, z, d_skip, norm_w = extra
        in_specs += [
            pl.BlockSpec((1, tb, d_ssd), lambda b, i: (b, pos(i), 0)),
            pl.BlockSpec((1, tb, d_ssd), lambda b, i: (b, pos(i), 0)),
            _resident((1, d_ssd)),
            _resident((1, d_ssd)),
        ]
        args += [yf, z, jnp.repeat(d_skip, SSD_HEADDIM)[None], norm_w[None]]
        y_dtype = BF16
        scratch.append(pltpu.VMEM((tb, d_ssd), F32))
    else:
        y_dtype = BF16
    return pl.pallas_call(
        functools.partial(_ssd_kernel, rev, nblk, tb, heads),
        grid=(bsz, nblk),
        in_specs=in_specs,
        out_specs=[pl.BlockSpec((1, tb, d_ssd), lambda b, i: (b, pos(i), 0)), st_spec],
        out_shape=[jax.ShapeDtypeStruct((bsz, s, d_ssd), y_dtype), st_shape],
        scratch_shapes=scratch,
        compiler_params=_cparams("parallel", "arbitrary"),
        name="ssd_bwd" if rev else "ssd_fwd",
    )(*args)


def _ssd_bidir(z, xs, dt_raw, p, s0_f, s0_b, tb):
    yf, s_f = _ssd_dir(False, xs, dt_raw, p["dt_bias"], p["a_log"], s0_f, tb)
    y, s_b = _ssd_dir(True, xs, dt_raw, p["dt_bias"], p["a_log"], s0_b, tb,
                      extra=(yf, z, p["d_skip"], p["norm_w"]))
    return y, s_f, s_b


def _shift_rows(x, s, fill, up):
    c = x.shape[0]
    rolled = pltpu.roll(x, (c - s) if up else s, 0)
    row = lax.broadcasted_iota(jnp.int32, x.shape, 0)
    keep = (row < c - s) if up else (row >= s)
    return jnp.where(keep, rolled, fill)


def _lru_kernel(nr, nc, rb, continuous, lx_ref, lg_ref, cw_ref, cb_ref, wg_ref, bg_ref, spl_ref, h0_ref,
                g_ref, hfin_ref, xp_ref, a_ref, b_ref):
    cwid = lx_ref.shape[-1]
    xin = lx_ref[0].astype(F32)
    xp_ref[2:nr + 2] = xin
    if continuous:
        for k in range(2):
            xp_ref[k] = _shift_rows(xin[nr - 2 + k], 1, 0.0, up=False)
        xp_ref[nr + 2] = _shift_rows(xin[0], 1, 0.0, up=True)
    else:
        xp_ref[0:2] = jnp.zeros((2, nc, cwid), F32)
        xp_ref[nr + 2:nr + 3] = jnp.zeros((1, nc, cwid), F32)

    def gates(t, _):
        r0 = pl.multiple_of(t * rb, rb)
        xc = cb_ref[...] + cw_ref[0:1] * xp_ref[pl.ds(r0, rb)]
        for k in range(1, CONV_K):
            xc = xc + cw_ref[k:k + 1] * xp_ref[pl.ds(r0 + k, rb)]
        xc2 = xc.reshape(rb * nc, cwid)
        gt = jnp.dot(xc2.astype(BF16), wg_ref[0], preferred_element_type=F32) + bg_ref[0]
        x_half = 0.5 * xc2
        for d in range(2):
            t_r = jnp.tanh(gt[:, (2 * d) * cwid:(2 * d + 1) * cwid])
            t_i = jnp.tanh(gt[:, (2 * d + 1) * cwid:(2 * d + 2) * cwid])
            c_half = spl_ref[0, d:d + 1]
            nla = c_half * t_r + c_half
            a = jnp.exp2(nla * (-LOG2E))
            one_m_a2 = jnp.tanh(nla) * (a * a + 1.0)
            root = one_m_a2 * lax.rsqrt(jnp.maximum(one_m_a2, F32_TINY))
            a_ref[d, pl.ds(r0, rb)] = a.reshape(rb, nc, cwid)
            b_ref[d, pl.ds(r0, rb)] = (root * (t_i + 1.0) * x_half).reshape(rb, nc, cwid)
        return 0

    lax.fori_loop(0, nr // rb, gates, 0)

    hin = []
    for d in range(2):
        def totals(t, carry):
            h, p = carry
            r = (nr - 1 - t) if d else t
            a = a_ref[d, r]
            return a * h + b_ref[d, r], a * p

        h_l, p_l = lax.fori_loop(0, nr, totals, (jnp.zeros((nc, cwid), F32), jnp.ones((nc, cwid), F32)),
                                 unroll=4)
        sa, sb = p_l, h_l
        s = 1
        while s < nc:
            sb = sa * _shift_rows(sb, s, 0.0, up=bool(d)) + sb
            sa = sa * _shift_rows(sa, s, 1.0, up=bool(d))
            s *= 2
        h0 = h0_ref[0, d:d + 1]
        after = sa * h0 + sb
        hin.append(_shift_rows(after, 1, h0, up=bool(d)) if nc > 1 else jnp.broadcast_to(h0, (nc, cwid)))
        last = 0 if d else nc - 1
        hfin_ref[0, d:d + 1] = after[last:last + 1]

    def fwd(r, h):
        h = a_ref[0, r] * h + b_ref[0, r]
        b_ref[0, r] = h
        return h

    lax.fori_loop(0, nr, fwd, hin[0], unroll=4)

    def bwd(t, h):
        r = nr - 1 - t
        h = a_ref[1, r] * h + b_ref[1, r]
        v = lg_ref[0, r].astype(F32)
        t = jnp.tanh(v * (GELU_K + (GELU_K * 0.044715) * (v * v)))
        sv = (h + b_ref[0, r]) * (0.5 * v)
        g_ref[0, r] = (sv + sv * t).astype(BF16)
        return h

    lax.fori_loop(0, nr, bwd, hin[1], unroll=4)


def _lru(lx, lg, p, h0, rb, continuous):
    bsz, nr, nc, d = lx.shape
    cwid = p["spl"].shape[-1]
    ng = d // cwid
    return pl.pallas_call(
        functools.partial(_lru_kernel, nr, nc, rb, continuous),
        grid=(bsz, ng),
        in_specs=[
            pl.BlockSpec((1, nr, nc, cwid), lambda b, j: (b, 0, 0, j)),
            pl.BlockSpec((1, nr, nc, cwid), lambda b, j: (b, 0, 0, j)),
            pl.BlockSpec((CONV_K, cwid), lambda b, j: (0, j)),
            pl.BlockSpec((1, cwid), lambda b, j: (0, j)),
            pl.BlockSpec((1, cwid, 4 * cwid), lambda b, j: (j, 0, 0)),
            pl.BlockSpec((1, 1, 4 * cwid), lambda b, j: (j, 0, 0)),
            pl.BlockSpec((1, 2, cwid), lambda b, j: (j, 0, 0)),
            pl.BlockSpec((1, 2, cwid), lambda b, j: (b, 0, j)),
        ],
        out_specs=[
            pl.BlockSpec((1, nr, nc, cwid), lambda b, j: (b, 0, 0, j)),
            pl.BlockSpec((1, 2, cwid), lambda b, j: (b, 0, j)),
        ],
        out_shape=[jax.ShapeDtypeStruct((bsz, nr, nc, d), BF16), jax.ShapeDtypeStruct((bsz, 2, d), F32)],
        scratch_shapes=[
            pltpu.VMEM((nr + 3, nc, cwid), F32),
            pltpu.VMEM((2, nr, nc, cwid), F32),
            pltpu.VMEM((2, nr, nc, cwid), F32),
        ],
        compiler_params=_cparams("parallel", "arbitrary"),
        name="rglru",
    )(lx, lg, p["conv_w"], p["conv_b"][None], p["wg"], p["bg"], p["spl"], h0)


def _lru_params(conv_w, conv_b, rw, rb_, iw, ib, lam, cwid):
    d = conv_w.shape[-1]
    ng = d // cwid
    nb = cwid // LRU_BW

    def block_diag(w):
        w = w.reshape(ng, nb, LRU_BW, LRU_BW)
        eye = jnp.eye(nb, dtype=w.dtype)
        return jnp.einsum("gakj,ab->gakbj", w, eye).reshape(ng, cwid, cwid)

    wg = jnp.concatenate([block_diag(rw[0]), block_diag(iw[0]), block_diag(rw[1]), block_diag(iw[1])], axis=-1)
    bg = 0.5 * jnp.concatenate([rb_[0].reshape(ng, cwid), ib[0].reshape(ng, cwid),
                                rb_[1].reshape(ng, cwid), ib[1].reshape(ng, cwid)], axis=-1)[:, None, :]
    spl = (0.5 * RGLRU_C) * jax.nn.softplus(-lam.astype(F32)).reshape(2, ng, cwid).transpose(1, 0, 2)
    return dict(conv_w=conv_w, conv_b=conv_b, wg=(0.5 * wg).astype(BF16), bg=bg, spl=spl)


def _swiglu_acc(h, w1_ref, w3_ref, w2_ref, nf):
    f = w1_ref.shape[-1]
    step = -(-f // (nf * MXU_DEPTH)) * MXU_DEPTH
    edges = list(range(0, f, step)) + [f]
    acc = None
    for lo, hi in zip(edges[:-1], edges[1:]):
        a = jnp.dot(h, w1_ref[:, lo:hi], preferred_element_type=F32)
        b = jnp.dot(h, w3_ref[:, lo:hi], preferred_element_type=F32)
        pj = (a * jax.nn.sigmoid(a) * b).astype(BF16)
        o = jnp.dot(pj, w2_ref[lo:hi, :], preferred_element_type=F32)
        acc = o if acc is None else acc + o
    return acc


def _outproj_kernel(route, final, y_ref, g_ref, x_ref, mod_ref, lnw_ref, wo_ref, n2w_ref, *rest):
    if route:
        rw_ref, x1_ref, h2_ref, rt_ref = rest
    elif final:
        w1_ref, w3_ref, w2_ref, fw_ref, o_ref = rest
    else:
        w1_ref, w3_ref, w2_ref, o_ref = rest
    m = mod_ref[0]
    gl = (_rms(g_ref[0].astype(F32)) * lnw_ref[...]).astype(BF16)
    dy = y_ref.shape[-1]
    proj = (jnp.dot(y_ref[0], wo_ref[0:dy], preferred_element_type=F32)
            + jnp.dot(gl, wo_ref[dy:], preferred_element_type=F32))
    x1 = x_ref[0] + m[2:3] * proj
    h2 = _rms(x1) * n2w_ref[...] * (1.0 + m[4:5]) + m[3:4]
    if not route:
        out = x1 + m[5:6] * _swiglu_acc(h2.astype(BF16), w1_ref, w3_ref, w2_ref, 2)
        if final:
            out = _rms(out) * fw_ref[...]
        o_ref[0] = out
        return
    x1_ref[0] = x1
    h2_ref[0] = _pack_pairs(h2)
    h_hi = h2.astype(BF16)
    h_lo = (h2 - h_hi.astype(F32)).astype(BF16)
    r_hi = jnp.dot(h_hi, rw_ref[...], preferred_element_type=F32)
    logits = (r_hi[:, :LANE] + r_hi[:, LANE:]
              + jnp.dot(h_lo, rw_ref[:, :LANE], preferred_element_type=F32))
    lane = lax.broadcasted_iota(jnp.int32, logits.shape, 1).astype(F32)
    neg = -jnp.inf
    logits = jnp.where(lane < N_EXPERTS, logits, neg)
    m1 = jnp.max(logits, axis=-1, keepdims=True)
    i1 = jnp.min(jnp.where(logits == m1, lane, float(LANE)), axis=-1, keepdims=True)
    rest_l = jnp.where(lane == i1, neg, logits)
    m2 = jnp.max(rest_l, axis=-1, keepdims=True)
    i2 = jnp.min(jnp.where(rest_l == m2, lane, float(LANE)), axis=-1, keepdims=True)
    e = jnp.exp(m2 - m1)
    p1 = 1.0 / (1.0 + e)
    p2 = e / (1.0 + e)
    rt_ref[0] = jnp.where(lane == 0.0, i1, jnp.where(lane == 1.0, i2,
                          jnp.where(lane == 2.0, p1, jnp.where(lane == 3.0, p2, 0.0))))


def _outproj(y, g, x, mod_l, mod_row, lru_norm_w, w_out, norm2_w, tm, router_w=None, ffn_w=None, final_w=None):
    bsz, s, d = x.shape
    route = router_w is not None
    final = final_w is not None and not route
    in_specs = [
        pl.BlockSpec((1, tm, y.shape[-1]), lambda b, i: (b, i, 0)),
        pl.BlockSpec((1, tm, g.shape[-1]), lambda b, i: (b, i, 0)),
        pl.BlockSpec((1, tm, d), lambda b, i: (b, i, 0)),
        pl.BlockSpec((1, N_MOD, d), lambda b, i: (mod_row(b), 0, 0)),
        _resident((1, g.shape[-1])),
        _resident(w_out.shape),
        _resident((1, d)),
    ]
    args = [y, g, x, mod_l, lru_norm_w[None], w_out, norm2_w[None]]
    tok = pl.BlockSpec((1, tm, d), lambda b, i: (b, i, 0))
    tok_shape = jax.ShapeDtypeStruct((bsz, s, d), F32)
    if route:
        rw = jnp.zeros((d, LANE), F32).at[:, :N_EXPERTS].set(router_w)
        rw_hi = rw.astype(BF16)
        in_specs.append(_resident((d, 2 * LANE)))
        args.append(jnp.concatenate([rw_hi, (rw - rw_hi.astype(F32)).astype(BF16)], axis=1))
        out_specs = [tok, pl.BlockSpec((1, tm, d // 2), lambda b, i: (b, i, 0)),
                     pl.BlockSpec((1, tm, LANE), lambda b, i: (b, i, 0))]
        out_shape = [tok_shape, jax.ShapeDtypeStruct((bsz, s, d // 2), jnp.uint32),
                     jax.ShapeDtypeStruct((bsz, s, LANE), F32)]
    else:
        in_specs += [_resident(w.shape) for w in ffn_w]
        args += list(ffn_w)
        if final:
            in_specs.append(_resident((1, d)))
            args.append(final_w[None])
        out_specs, out_shape = tok, tok_shape
    return pl.pallas_call(
        functools.partial(_outproj_kernel, route, final),
        grid=(bsz, s // tm),
        in_specs=in_specs,
        out_specs=out_specs,
        out_shape=out_shape,
        compiler_params=_cparams("parallel", "arbitrary"),
        name="outproj_route" if route else "outproj_ffn",
    )(*args)


def _sc_worker_id():
    return lax.axis_index("subcore") * SC_CORES + lax.axis_index("core")


def _sc_mesh():
    return plsc.VectorSubcoreMesh(core_axis_name="core", subcore_axis_name="subcore")


def _sc_scatter2(x, i0, i1, n_rows):
    m, d = x.shape
    per_w = m // SC_WORKERS
    assert per_w % SC_WINDOW == 0

    @functools.partial(
        pl.kernel, out_type=jax.ShapeDtypeStruct((n_rows, d), x.dtype), mesh=_sc_mesh(),
        scratch_types=[pltpu.VMEM((SC_WINDOW,), jnp.int32), pltpu.VMEM((SC_WINDOW, d), x.dtype),
                       pltpu.SemaphoreType.DMA])
    def scatter(x_hbm, i0_hbm, i1_hbm, o_hbm, idx_v, rows_v, sem):
        wid = _sc_worker_id()

        @pl.loop(0, per_w // SC_WINDOW)
        def _(j):
            base = wid * per_w + j * SC_WINDOW
            pltpu.sync_copy(x_hbm.at[pl.ds(base, SC_WINDOW)], rows_v)
            for i_hbm in (i0_hbm, i1_hbm):
                pltpu.sync_copy(i_hbm.at[pl.ds(base, SC_WINDOW)], idx_v)
                pltpu.async_copy(rows_v, o_hbm.at[idx_v], sem).wait()

    return scatter(x, i0, i1)


def _sc_gather2(table, i0, i1):
    m = i0.shape[0]
    d = table.shape[1]
    per_w = m // SC_WORKERS
    assert per_w % SC_WINDOW == 0
    out = jax.ShapeDtypeStruct((m, d), table.dtype)

    @functools.partial(
        pl.kernel, out_type=(out, out), mesh=_sc_mesh(),
        scratch_types=[pltpu.VMEM((SC_WINDOW,), jnp.int32), pltpu.VMEM((SC_WINDOW, d), table.dtype),
                       pltpu.SemaphoreType.DMA])
    def gather(t_hbm, i0_hbm, i1_hbm, a_hbm, b_hbm, idx_v, rows_v, sem):
        wid = _sc_worker_id()

        @pl.loop(0, per_w // SC_WINDOW)
        def _(j):
            base = wid * per_w + j * SC_WINDOW
            for i_hbm, o_hbm in ((i0_hbm, a_hbm), (i1_hbm, b_hbm)):
                pltpu.sync_copy(i_hbm.at[pl.ds(base, SC_WINDOW)], idx_v)
                pltpu.async_copy(t_hbm.at[idx_v], rows_v, sem).wait()
                pltpu.sync_copy(rows_v, o_hbm.at[pl.ds(base, SC_WINDOW)])

    return gather(table, i0, i1)


def _experts_kernel(nf, te_ref, tv_ref, h_ref, w1_ref, w3_ref, w2_ref, o_ref):
    valid = tv_ref[pl.program_id(0)]

    @pl.when(valid > 0)
    def _():
        row = lax.broadcasted_iota(jnp.int32, h_ref.shape, 0)
        words = jnp.where(row < valid, h_ref[...], jnp.uint32(0))
        lo, hi = _unpack_pairs(words)
        h = jnp.concatenate([lo, hi], axis=1).astype(BF16)
        o_ref[...] = _pack_pairs(_swiglu_acc(h, w1_ref.at[0], w3_ref.at[0], w2_ref.at[0], nf))

    @pl.when(valid <= 0)
    def _():
        o_ref[...] = jnp.zeros_like(o_ref)


def _experts(h_sorted, tile_expert, tile_valid, w1, w3, w2, tile):
    n, dw = h_sorted.shape
    d, f = w1.shape[1:]
    grid_spec = pltpu.PrefetchScalarGridSpec(
        num_scalar_prefetch=2,
        grid=(n // tile,),
        in_specs=[
            pl.BlockSpec((tile, dw), lambda i, te, tv: (i, 0)),
            pl.BlockSpec((1, d, f), lambda i, te, tv: (te[i], 0, 0), pipeline_mode=pl.Buffered(1)),
            pl.BlockSpec((1, d, f), lambda i, te, tv: (te[i], 0, 0), pipeline_mode=pl.Buffered(1)),
            pl.BlockSpec((1, f, d), lambda i, te, tv: (te[i], 0, 0), pipeline_mode=pl.Buffered(1)),
        ],
        out_specs=pl.BlockSpec((tile, dw), lambda i, te, tv: (i, 0)),
    )
    return pl.pallas_call(
        functools.partial(_experts_kernel, 2),
        grid_spec=grid_spec,
        out_shape=jax.ShapeDtypeStruct((n, dw), jnp.uint32),
        compiler_params=_cparams("arbitrary"),
        name="experts",
    )(tile_expert, tile_valid, h_sorted, w1, w3, w2)


def _combine_kernel(x1_ref, ya_ref, yb_ref, rt_ref, mod_ref, fw_ref, o_ref):
    m = mod_ref[0]
    rt = rt_ref[0]
    a_lo, a_hi = _unpack_pairs(ya_ref[0])
    b_lo, b_hi = _unpack_pairs(yb_ref[0])
    p1, p2 = rt[:, 2:3], rt[:, 3:4]
    moe = jnp.concatenate([p1 * a_lo + p2 * b_lo, p1 * a_hi + p2 * b_hi], axis=1)
    o_ref[0] = _rms(x1_ref[0] + m[5:6] * moe) * fw_ref[...]


def _moe(x1, h2, route, mod_l, mod_row, w1, w3, w2, final_w, tm, tile):
    bsz, s, d = x1.shape
    n = bsz * s
    ex = route[..., :2].astype(jnp.int32).reshape(2 * n)
    onehot = (ex[:, None] == jnp.arange(N_EXPERTS, dtype=jnp.int32)[None, :]).astype(jnp.int32)
    csum = jnp.cumsum(onehot, axis=0)
    cnt = csum[-1]
    rank = jnp.sum((csum - onehot) * onehot, axis=1)
    padded = ((cnt + tile - 1) // tile) * tile
    ends = jnp.cumsum(padded)
    pos = (ends - padded)[ex] + rank
    n_rows = 2 * n + N_EXPERTS * tile
    tile_start = jnp.arange(n_rows // tile, dtype=jnp.int32) * tile
    tile_expert = jnp.minimum(jnp.searchsorted(ends, tile_start, side="right"), N_EXPERTS - 1).astype(jnp.int32)
    group_end = (ends - padded + cnt)[tile_expert]
    tile_valid = jnp.clip(group_end - tile_start, 0, tile).astype(jnp.int32)

    pos2 = pos.reshape(n, 2)
    dw = h2.shape[-1]
    h_sorted = _sc_scatter2(h2.reshape(n, dw), pos2[:, 0], pos2[:, 1], n_rows)
    y_sorted = _experts(h_sorted, tile_expert, tile_valid, w1, w3, w2, tile)
    ya, yb = _sc_gather2(y_sorted, pos2[:, 0], pos2[:, 1])
    ya = ya.reshape(bsz, s, dw)
    yb = yb.reshape(bsz, s, dw)
    tok = pl.BlockSpec((1, tm, d), lambda b, i: (b, i, 0))
    words = pl.BlockSpec((1, tm, dw), lambda b, i: (b, i, 0))
    return pl.pallas_call(
        _combine_kernel,
        grid=(bsz, s // tm),
        in_specs=[tok, words, words,
                  pl.BlockSpec((1, tm, LANE), lambda b, i: (b, i, 0)),
                  pl.BlockSpec((1, N_MOD, d), lambda b, i: (mod_row(b), 0, 0)),
                  _resident((1, d))],
        out_specs=tok,
        out_shape=jax.ShapeDtypeStruct((bsz, s, d), F32),
        compiler_params=_cparams("parallel", "arbitrary"),
        name="moe_combine",
    )(x1, ya, yb, route, mod_l, final_w[None])


def _pick_tile(s, pref):
    t = min(s, pref)
    while s % t:
        t //= 2
    return t


def kernel(x, c, ctx, c_ctx, mod_w, mod_b, norm1_w, norm2_w, w_in, ssd_conv_w, ssd_conv_b, ssd_dt_bias,
           ssd_a_log, ssd_d, ssd_norm_w, lru_conv_w, lru_conv_b, lru_rw, lru_rb, lru_iw, lru_ib, lru_lambda,
           lru_norm_w, w_out, ffn_w1, ffn_w3, ffn_w2, router_w, moe_w1, moe_w3, moe_w2, final_norm_w):
    depth = w_in.shape[0]
    bsz, seq, d = x.shape
    lc = ctx.shape[1]
    rows = seq // GRID_W
    heads = ssd_dt_bias.shape[-1]
    d_ssd = heads * SSD_HEADDIM
    d_xbc = ssd_conv_w.shape[-1]
    d_lru = lru_conv_w.shape[-1]
    splits = (d_ssd, d_xbc, d_lru, d_lru)
    off_dt = d_ssd + d_xbc
    off_lx = off_dt + 2 * heads
    ctx_cols = 8
    ctx_rows = lc // ctx_cols
    lru_cw = 256

    mod = _modulation(c, c_ctx, mod_w, mod_b)
    lat_row = lambda b: b
    ctx_row = lambda b: bsz
    tm_lat = _pick_tile(seq, 512)
    tm_ctx = _pick_tile(lc, 512)
    tb_lat = _pick_tile(seq, 1024)
    tb_ctx = _pick_tile(lc, 512)
    gw = d_ssd // SSD_GROUPS

    xc = ctx
    for l in range(depth):
        last = l == depth - 1
        mod_l = mod[l]
        w_l = w_in[l].astype(BF16)
        w_perm = jnp.concatenate([w_l[:, :off_dt], w_l[:, off_lx:], w_l[:, off_dt:off_lx],
                                  jnp.zeros((d, LANE - 2 * heads), BF16)], axis=1)
        ssd_p = dict(dt_bias=ssd_dt_bias[l], a_log=ssd_a_log[l], d_skip=ssd_d[l], norm_w=ssd_norm_w[l])
        lru_p = _lru_params(lru_conv_w[l], lru_conv_b[l], lru_rw[l], lru_rb[l], lru_iw[l], lru_ib[l],
                            lru_lambda[l], lru_cw)
        wo = w_out[l].astype(BF16)

        zc, xbcc, lxc, lgc, dtc = _inproj(xc, mod_l, ctx_row, norm1_w[l], w_perm, ssd_conv_w[l], ssd_conv_b[l],
                                          splits, 2 * heads, tm_ctx)
        zl, xbcl, lxl, lgl, dtl = _inproj(x, mod_l, lat_row, norm1_w[l], w_perm, ssd_conv_w[l], ssd_conv_b[l],
                                          splits, 2 * heads, tm_lat)

        zero_s = jnp.zeros((bsz, SSD_GROUPS, SSD_STATE, gw), F32)
        yc_ssd, s_f, s_b = _ssd_bidir(zc, xbcc, dtc, ssd_p, zero_s, zero_s, tb_ctx)
        yl_ssd, _, _ = _ssd_bidir(zl, xbcl, dtl, ssd_p, s_f, s_b, tb_lat)

        to_grid = lambda v: v.reshape(bsz, ctx_cols, ctx_rows, d_lru).transpose(0, 2, 1, 3)
        gc, h_fin = _lru(to_grid(lxc), to_grid(lgc), lru_p, jnp.zeros((bsz, 2, d_lru), F32), ctx_rows, True)
        gl, _ = _lru(lxl.reshape(bsz, rows, GRID_W, d_lru), lgl.reshape(bsz, rows, GRID_W, d_lru), lru_p,
                     h_fin, 8, False)
        gl = gl.reshape(bsz, seq, d_lru)

        if l % 2 == 0:
            w1, w3, w2 = (ffn_w1[l // 2].astype(BF16), ffn_w3[l // 2].astype(BF16), ffn_w2[l // 2].astype(BF16))
            x_next = _outproj(yl_ssd, gl, x, mod_l, lat_row, lru_norm_w[l], wo, norm2_w[l], tm_lat,
                              ffn_w=(w1, w3, w2), final_w=final_norm_w if last else None)
            if not last:
                gc = gc.transpose(0, 2, 1, 3).reshape(bsz, lc, d_lru)
                xc = _outproj(yc_ssd, gc, xc, mod_l, ctx_row, lru_norm_w[l], wo, norm2_w[l], tm_ctx,
                              ffn_w=(w1, w3, w2))
            x = x_next
        else:
            w1, w3, w2 = (moe_w1[l // 2].astype(BF16), moe_w3[l // 2].astype(BF16), moe_w2[l // 2].astype(BF16))
            assert last, "a routed layer that is not the last layer is not implemented"
            x1, h2, route = _outproj(yl_ssd, gl, x, mod_l, lat_row, lru_norm_w[l], wo, norm2_w[l], tm_lat,
                                     router_w=router_w[l // 2])
            x = _moe(x1, h2, route, mod_l, lat_row, w1, w3, w2, final_norm_w, tm_lat, 512)
    return x
```

```python
import functools

import jax
import jax.numpy as jnp
from jax import lax
from jax.experimental import pallas as pl
from jax.experimental.pallas import tpu as pltpu
from jax.experimental.pallas import tpu_sc as plsc

F32 = jnp.float32
BF16 = jnp.bfloat16
HI = lax.Precision.HIGHEST

EPS = 1e-6
N_MOD = 6
GRID_W = 64
SSD_HEADDIM = 64
SSD_GROUPS = 2
SSD_STATE = 128
SSD_CHUNK = 128
CONV_K = 4
LRU_BW = 64
RGLRU_C = 8.0
N_EXPERTS = 8
LOG2E = 1.4426950408889634
GELU_K = 0.7978845608028654
F32_TINY = float(jnp.finfo(jnp.float32).tiny)
HALO = 16
LANE = 128
MXU_DEPTH = 256
MOD_ROWS = 24
VMEM_LIMIT = 56 * 1024 * 1024
SC_CORES = 2
SC_WORKERS = 32
SC_WINDOW = 128


def _cparams(*sem):
    return pltpu.CompilerParams(dimension_semantics=sem, vmem_limit_bytes=VMEM_LIMIT)


def _resident(shape):
    nd = len(shape)
    return pl.BlockSpec(shape, lambda *_: (0,) * nd, pipeline_mode=pl.Buffered(1))


def _rms(x):
    return x * lax.rsqrt(jnp.mean(x * x, axis=-1, keepdims=True) + EPS)


def _pack_pairs(v):
    k = v.shape[1] // 2
    lo = lax.bitcast_convert_type(v[:, :k].astype(BF16).astype(F32), jnp.uint32)
    hi = lax.bitcast_convert_type(v[:, k:].astype(BF16).astype(F32), jnp.uint32)
    return hi | (lo >> 16)


def _unpack_pairs(w):
    lo = lax.bitcast_convert_type(w << 16, F32)
    hi = lax.bitcast_convert_type(w & jnp.uint32(0xFFFF0000), F32)
    return lo, hi


def _mod_kernel(s_ref, w_ref, b_ref, o_ref):
    s = s_ref[...]
    s = s * jax.nn.sigmoid(s)
    o_ref[0] = jnp.dot(s, w_ref[0], precision=HI, preferred_element_type=F32) + b_ref[0]


def _modulation(c, c_ctx, mod_w, mod_b):
    depth, d, n = mod_w.shape
    bsz = c.shape[0]
    assert bsz + 1 <= MOD_ROWS
    s = jnp.concatenate([c, c_ctx[None], jnp.zeros((MOD_ROWS - bsz - 1, d), F32)], axis=0)
    tn = n // 4
    out = pl.pallas_call(
        _mod_kernel,
        grid=(depth, n // tn),
        in_specs=[
            pl.BlockSpec((MOD_ROWS, d), lambda l, j: (0, 0)),
            pl.BlockSpec((1, d, tn), lambda l, j: (l, 0, j)),
            pl.BlockSpec((1, 1, tn), lambda l, j: (l, 0, j)),
        ],
        out_specs=pl.BlockSpec((1, MOD_ROWS, tn), lambda l, j: (l, 0, j)),
        out_shape=jax.ShapeDtypeStruct((depth, MOD_ROWS, n), F32),
        compiler_params=_cparams("arbitrary", "arbitrary"),
        name="modulation",
    )(s, mod_w, mod_b[:, None, :])
    return out.reshape(depth, MOD_ROWS, N_MOD, d)


def _inproj_kernel(splits, dt_w, nt, x_ref, xprev_ref, xnext_ref, mod_ref, nw_ref, w_ref, cw_ref, cb_ref,
                   z_ref, xs_ref, lx_ref, lg_ref, dt_ref):
    i = pl.program_id(1)
    m = mod_ref[0]
    tm = x_ref.shape[1]

    def norm_mod(v):
        return (_rms(v) * nw_ref[...] * (1.0 + m[1:2]) + m[0:1]).astype(BF16)

    h = norm_mod(x_ref[0])

    def proj(a, b):
        return jnp.dot(h, w_ref[:, a:b], preferred_element_type=F32)

    o = [0]
    for s in splits:
        o.append(o[-1] + s)
    h_ext = jnp.concatenate([norm_mod(xprev_ref[0]), h, norm_mod(xnext_ref[0])], axis=0)
    xe = jnp.dot(h_ext, w_ref[:, o[1]:o[2]], preferred_element_type=F32)
    row = lax.broadcasted_iota(jnp.int32, (tm + 2 * HALO, 1), 0)
    outside = jnp.logical_or(jnp.logical_and(row < HALO, i == 0),
                             jnp.logical_and(row >= tm + HALO, i == nt - 1))
    xe = jnp.where(outside, 0.0, xe)
    n_ext = tm + 2 * HALO

    acc = cb_ref[...] + cw_ref[2:3] * xe[HALO:HALO + tm]
    for k in (0, 1, 3):
        acc = acc + cw_ref[k:k + 1] * pltpu.roll(xe, (2 - k) % n_ext, 0)[HALO:HALO + tm]
    xs_ref[0] = (acc * jax.nn.sigmoid(acc)).astype(BF16)

    z_ref[0] = proj(o[0], o[1]).astype(BF16)
    lx_ref[0] = proj(o[2], o[3]).astype(BF16)
    lg_dt = proj(o[3], o[4] + LANE)
    lg_ref[0] = lg_dt[:, :splits[3]].astype(BF16)
    dt_ref[0] = lg_dt[:, splits[3]:splits[3] + dt_w]


def _inproj(x, mod_l, mod_row, norm_w, w_perm, conv_w, conv_b, splits, dt_w, tm):
    bsz, s, d = x.shape
    n_all = w_perm.shape[1]
    nt = s // tm
    nh = tm // HALO
    outs =[jax.ShapeDtypeStruct((bsz, s, w), BF16) for w in splits] + [jax.ShapeDtypeStruct((bsz, s, dt_w), F32)]
    return pl.pallas_call(
        functools.partial(_inproj_kernel, splits, dt_w, nt),
        grid=(bsz, nt),
        in_specs=[
            pl.BlockSpec((1, tm, d), lambda b, i: (b, i, 0)),
            pl.BlockSpec((1, HALO, d), lambda b, i: (b, jnp.maximum(i * nh - 1, 0), 0)),
            pl.BlockSpec((1, HALO, d), lambda b, i: (b, jnp.minimum((i + 1) * nh, s // HALO - 1), 0)),
            pl.BlockSpec((1, N_MOD, d), lambda b, i: (mod_row(b), 0, 0)),
            _resident((1, d)),
            _resident((d, n_all)),
            _resident((CONV_K, splits[1])),
            _resident((1, splits[1])),
        ],
        out_specs=[pl.BlockSpec((1, tm, w), lambda b, i: (b, i, 0)) for w in splits]
        + [pl.BlockSpec((1, tm, dt_w), lambda b, i: (b, i, 0))],
        out_shape=outs,
        compiler_params=_cparams("parallel", "arbitrary"),
        name="inproj",
    )(x, x, x, mod_l, norm_w[None], w_perm, conv_w, conv_b[None])


def _softplus(x):
    return jnp.maximum(x, 0.0) + jnp.log1p(jnp.exp(-jnp.abs(x)))


def _ssd_kernel(rev, nblk, tb, heads, *refs):
    if rev:
        (xs_ref, dt_ref, dtb_ref, alog_ref, s0_ref,
         yf_ref, z_ref, dsk_ref, nw_ref, y_ref, sfin_ref, st_ref, yb_ref) = refs
    else:
        (xs_ref, dt_ref, dtb_ref, alog_ref, s0_ref, y_ref, sfin_ref, st_ref) = refs
        yb_ref = y_ref.at[0]
    i = pl.program_id(1)
    d_ssd = heads * SSD_HEADDIM
    gn = SSD_GROUPS * SSD_STATE
    gw = d_ssd // SSD_GROUPS
    ck = SSD_CHUNK

    @pl.when(i == 0)
    def _():
        st_ref[...] = s0_ref[0]

    d0 = heads if rev else 0
    dt = _softplus(dt_ref[0] + dtb_ref[...])[:, d0:d0 + heads]
    a_neg = -jnp.exp(alog_ref[...])[:, d0:d0 + heads]
    da = dt * a_neg
    ldt = jnp.where(dt > 0.0, jnp.log(dt), -1e30)

    ri = lax.broadcasted_iota(jnp.int32, (ck, ck), 0)
    ci = lax.broadcasted_iota(jnp.int32, (ck, ck), 1)
    tri = (ci >= ri) if rev else (ci <= ri)
    tri_b = jnp.where(tri, 1.0, 0.0).astype(BF16)
    tri_tb = jnp.where((ri >= ci) if rev else (ri <= ci), 1.0, 0.0).astype(BF16)
    eye_b = jnp.where(ri == ci, 1.0, 0.0).astype(BF16)
    tri_mask_b = tri_b > 0
    zero_b = jnp.zeros((ck, ck), BF16)

    def terms(q):
        hi = q.astype(BF16)
        r1 = q - hi.astype(F32)
        mid = r1.astype(BF16)
        return [hi, mid, (r1 - mid.astype(F32)).astype(BF16)]

    nck = tb // ck
    side = lambda q: jnp.concatenate([q[c * ck:(c + 1) * ck] for c in range(nck)], axis=1)
    da_s, ldt_s, dt_s = side(da), side(ldt), side(dt)
    cs_all = jnp.dot(jnp.concatenate([tri_b] * 3, axis=1), jnp.concatenate(terms(da_s), axis=0),
                     preferred_element_type=F32)
    rt_all = lax.dot_general(jnp.concatenate(terms(da_s) + terms(-ldt_s), axis=0),
                             jnp.concatenate([tri_tb] * 3 + [eye_b] * 3, axis=0),
                             (((0,), (0,)), ((), ())), preferred_element_type=F32)
    end = 0 if rev else ck - 1
    ecs_all = jnp.exp(cs_all)
    wgt_all = jnp.exp(cs_all[end:end + 1] - cs_all) * dt_s
    cs2_all = cs_all * LOG2E
    rt2_all = rt_all * LOG2E
    eh = lax.broadcasted_iota(jnp.int32, (heads, d_ssd), 0)
    ec = lax.broadcasted_iota(jnp.int32, (heads, d_ssd), 1)
    expand = jnp.where(ec // SSD_HEADDIM == eh, 1.0, 0.0).astype(BF16)
    expand2 = jnp.concatenate([expand, expand], axis=0)
    lane = lax.broadcasted_iota(jnp.int32, (ck, LANE), 1)

    def widen(q):
        hi = q.astype(BF16)
        lo = (q - hi.astype(F32)).astype(BF16)
        return jnp.dot(jnp.concatenate([hi, lo], axis=1), expand2, preferred_element_type=F32)

    order = range(tb // ck - 1, -1, -1) if rev else range(tb // ck)
    for c in order:
        r0 = c * ck
        x_bf = xs_ref[0, r0:r0 + ck, :d_ssd]
        b_bf = xs_ref[0, r0:r0 + ck, d_ssd:d_ssd + gn]
        c_bf = xs_ref[0, r0:r0 + ck, d_ssd + gn:]
        hs = slice(c * heads, (c + 1) * heads)
        ecs_w = widen(ecs_all[:, hs])
        xw = (x_bf.astype(F32) * widen(wgt_all[:, hs])).astype(BF16)
        cs2 = cs2_all[:, hs]
        r_t2 = rt2_all[hs, :]
        for g in range(SSD_GROUPS):
            bg = b_bf[:, g * SSD_STATE:(g + 1) * SSD_STATE]
            cg = c_bf[:, g * SSD_STATE:(g + 1) * SSD_STATE]
            cb = lax.dot_general(cg, bg, (((1,), (1,)), ((), ())),
                                 preferred_element_type=F32).astype(BF16)
            s_in = st_ref[g]
            y_off = jnp.dot(cg, s_in.astype(BF16), preferred_element_type=F32) * ecs_w[:, g * gw:(g + 1) * gw]
            new = lax.dot_general(bg, xw[:, g * gw:(g + 1) * gw], (((0,), (0,)), ((), ())),
                                  preferred_element_type=F32)
            st_ref[g] = ecs_w[end:end + 1, g * gw:(g + 1) * gw] * s_in + new
            for p in range(gw // LANE):
                ha = (g * gw + p * LANE) // SSD_HEADDIM
                lm = []
                for h in (ha, ha + 1):
                    seg2 = cs2[:, h:h + 1] - r_t2[h:h + 1, :]
                    lm.append(jnp.where(tri_mask_b, jnp.exp2(seg2).astype(BF16), zero_b) * cb)
                lhs = jnp.concatenate(lm, axis=1)
                xpair = x_bf[:, g * gw + p * LANE:g * gw + (p + 1) * LANE]
                zero = jnp.zeros_like(xpair)
                rhs = jnp.concatenate([jnp.where(lane < SSD_HEADDIM, xpair, zero),
                                       jnp.where(lane >= SSD_HEADDIM, xpair, zero)], axis=0)
                col = g * gw + p * LANE
                yb_ref[r0:r0 + ck, col:col + LANE] = (
                    jnp.dot(lhs, rhs, preferred_element_type=F32) + y_off[:, p * LANE:(p + 1) * LANE]
                ).astype(yb_ref.dtype)

    if rev:
        z = z_ref[0].astype(F32)
        x_all = xs_ref[0, :, :d_ssd].astype(F32)
        tot_y = (yb_ref[...] + yf_ref[0].astype(F32) + dsk_ref[...] * x_all) * (z * jax.nn.sigmoid(z))
        y_ref[0] = (_rms(tot_y) * nw_ref[...]).astype(BF16)

    @pl.when(i == nblk - 1)
    def _():
        sfin_ref[0] = st_ref[...]


def _ssd_dir(rev, xs, dt_raw, dt_bias, a_log, s0, tb, extra=None):
    bsz, s, dxbc = xs.shape
    heads = dt_raw.shape[-1] // 2
    d_ssd = heads * SSD_HEADDIM
    gw = d_ssd // SSD_GROUPS
    nblk = s // tb
    pos = (lambda i: nblk - 1 - i) if rev else (lambda i: i)
    in_specs = [
        pl.BlockSpec((1, tb, dxbc), lambda b, i: (b, pos(i), 0)),
        pl.BlockSpec((1, tb, 2 * heads), lambda b, i: (b, pos(i), 0)),
        _resident((1, 2 * heads)),
        _resident((1, 2 * heads)),
        pl.BlockSpec((1, SSD_GROUPS, SSD_STATE, gw), lambda b, i: (b, 0, 0, 0)),
    ]
    args = [xs, dt_raw, dt_bias.reshape(1, -1), a_log.reshape(1, -1), s0]
    st_shape = jax.ShapeDtypeStruct((bsz, SSD_GROUPS, SSD_STATE, gw), F32)
    st_spec = pl.BlockSpec((1, SSD_GROUPS, SSD_STATE, gw), lambda b, i: (b, 0, 0, 0))
    scratch = [pltpu.VMEM((SSD_GROUPS, SSD_STATE, gw), F32)]
    if rev:
        yf, z, d_skip, norm_w = extra
        in_specs += [
            pl.BlockSpec((1, tb, d_ssd), lambda b, i: (b, pos(i), 0)),
            pl.BlockSpec((1, tb, d_ssd), lambda b, i: (b, pos(i), 0)),
            _resident((1, d_ssd)),
            _resident((1, d_ssd)),
        ]
        args += [yf, z, jnp.repeat(d_skip, SSD_HEADDIM)[None], norm_w[None]]
        y_dtype = BF16
        scratch.append(pltpu.VMEM((tb, d_ssd), F32))
    else:
        y_dtype = BF16
    return pl.pallas_call(
        functools.partial(_ssd_kernel, rev, nblk, tb, heads),
        grid=(bsz, nblk),
        in_specs=in_specs,
        out_specs=[pl.BlockSpec((1, tb, d_ssd), lambda b, i: (b, pos(i), 0)), st_spec],
        out_shape=[jax.ShapeDtypeStruct((bsz, s, d_ssd), y_dtype), st_shape],
        scratch_shapes=scratch,
        compiler_params=_cparams("parallel", "arbitrary"),
        name="ssd_bwd" if rev else "ssd_fwd",
    )(*args)


def _ssd_bidir(z, xs, dt_raw, p, s0_f, s0_b, tb):
    yf, s_f = _ssd_dir(False, xs, dt_raw, p["dt_bias"], p["a_log"], s0_f, tb)
    y, s_b = _ssd_dir(True, xs, dt_raw, p["dt_bias"], p["a_log"], s0_b, tb,
                      extra=(yf, z, p["d_skip"], p["norm_w"]))
    return y, s_f, s_b


def _shift_rows(x, s, fill, up):
    c = x.shape[0]
    rolled = pltpu.roll(x, (c - s) if up else s, 0)
    row = lax.broadcasted_iota(jnp.int32, x.shape, 0)
    keep = (row < c - s) if up else (row >= s)
    return jnp.where(keep, rolled, fill)


def _lru_kernel(nr, nc, rb, continuous, lx_ref, lg_ref, cw_ref, cb_ref, wg_ref, bg_ref, spl_ref, h0_ref,
                g_ref, hfin_ref, xp_ref, a_ref, b_ref):
    cwid = lx_ref.shape[-1]
    xin = lx_ref[0].astype(F32)
    xp_ref[2:nr + 2] = xin
    if continuous:
        for k in range(2):
            xp_ref[k] = _shift_rows(xin[nr - 2 + k], 1, 0.0, up=False)
        xp_ref[nr + 2] = _shift_rows(xin[0], 1, 0.0, up=True)
    else:
        xp_ref[0:2] = jnp.zeros((2, nc, cwid), F32)
        xp_ref[nr + 2:nr + 3] = jnp.zeros((1, nc, cwid), F32)

    def gates(t, _):
        r0 = pl.multiple_of(t * rb, rb)
        xc = cb_ref[...] + cw_ref[0:1] * xp_ref[pl.ds(r0, rb)]
        for k in range(1, CONV_K):
            xc = xc + cw_ref[k:k + 1] * xp_ref[pl.ds(r0 + k, rb)]
        xc2 = xc.reshape(rb * nc, cwid)
        gt = jnp.dot(xc2.astype(BF16), wg_ref[0], preferred_element_type=F32) + bg_ref[0]
        x_half = 0.5 * xc2
        for d in range(2):
            t_r = jnp.tanh(gt[:, (2 * d) * cwid:(2 * d + 1) * cwid])
            t_i = jnp.tanh(gt[:, (2 * d + 1) * cwid:(2 * d + 2) * cwid])
            c_half = spl_ref[0, d:d + 1]
            nla = c_half * t_r + c_half
            a = jnp.exp2(nla * (-LOG2E))
            one_m_a2 = jnp.tanh(nla) * (a * a + 1.0)
            root = one_m_a2 * lax.rsqrt(jnp.maximum(one_m_a2, F32_TINY))
            a_ref[d, pl.ds(r0, rb)] = a.reshape(rb, nc, cwid)
            b_ref[d, pl.ds(r0, rb)] = (root * (t_i + 1.0) * x_half).reshape(rb, nc, cwid)
        return 0

    lax.fori_loop(0, nr // rb, gates, 0)

    hin = []
    for d in range(2):
        def totals(t, carry):
            h, p = carry
            r = (nr - 1 - t) if d else t
            a = a_ref[d, r]
            return a * h + b_ref[d, r], a * p

        h_l, p_l = lax.fori_loop(0, nr, totals, (jnp.zeros((nc, cwid), F32), jnp.ones((nc, cwid), F32)),
                                 unroll=4)
        sa, sb = p_l, h_l
        s = 1
        while s < nc:
            sb = sa * _shift_rows(sb, s, 0.0, up=bool(d)) + sb
            sa = sa * _shift_rows(sa, s, 1.0, up=bool(d))
            s *= 2
        h0 = h0_ref[0, d:d + 1]
        after = sa * h0 + sb
        hin.append(_shift_rows(after, 1, h0, up=bool(d)) if nc > 1 else jnp.broadcast_to(h0, (nc, cwid)))
        last = 0 if d else nc - 1
        hfin_ref[0, d:d + 1] = after[last:last + 1]

    def fwd(r, h):
        h = a_ref[0, r] * h + b_ref[0, r]
        b_ref[0, r] = h
        return h

    lax.fori_loop(0, nr, fwd, hin[0], unroll=4)

    def bwd(t, h):
        r = nr - 1 - t
        h = a_ref[1, r] * h + b_ref[1, r]
        v = lg_ref[0, r].astype(F32)
        t = jnp.tanh(v * (GELU_K + (GELU_K * 0.044715) * (v * v)))
        sv = (h + b_ref[0, r]) * (0.5 * v)
        g_ref[0, r] = (sv + sv * t).astype(BF16)
        return h

    lax.fori_loop(0, nr, bwd, hin[1], unroll=4)


def _lru(lx, lg, p, h0, rb, continuous):
    bsz, nr, nc, d = lx.shape
    cwid = p["spl"].shape[-1]
    ng = d // cwid
    return pl.pallas_call(
        functools.partial(_lru_kernel, nr, nc, rb, continuous),
        grid=(bsz, ng),
        in_specs=[
            pl.BlockSpec((1, nr, nc, cwid), lambda b, j: (b, 0, 0, j)),
            pl.BlockSpec((1, nr, nc, cwid), lambda b, j: (b, 0, 0, j)),
            pl.BlockSpec((CONV_K, cwid), lambda b, j: (0, j)),
            pl.BlockSpec((1, cwid), lambda b, j: (0, j)),
            pl.BlockSpec((1, cwid, 4 * cwid), lambda b, j: (j, 0, 0)),
            pl.BlockSpec((1, 1, 4 * cwid), lambda b, j: (j, 0, 0)),
            pl.BlockSpec((1, 2, cwid), lambda b, j: (j, 0, 0)),
            pl.BlockSpec((1, 2, cwid), lambda b, j: (b, 0, j)),
        ],
        out_specs=[
            pl.BlockSpec((1, nr, nc, cwid), lambda b, j: (b, 0, 0, j)),
            pl.BlockSpec((1, 2, cwid), lambda b, j: (b, 0, j)),
        ],
        out_shape=[jax.ShapeDtypeStruct((bsz, nr, nc, d), BF16), jax.ShapeDtypeStruct((bsz, 2, d), F32)],
        scratch_shapes=[
            pltpu.VMEM((nr + 3, nc, cwid), F32),
            pltpu.VMEM((2, nr, nc, cwid), F32),
            pltpu.VMEM((2, nr, nc, cwid), F32),
        ],
        compiler_params=_cparams("parallel", "arbitrary"),
        name="rglru",
    )(lx, lg, p["conv_w"], p["conv_b"][None], p["wg"], p["bg"], p["spl"], h0)


def _lru_params(conv_w, conv_b, rw, rb_, iw, ib, lam, cwid):
    d = conv_w.shape[-1]
    ng = d // cwid
    nb = cwid // LRU_BW

    def block_diag(w):
        w = w.reshape(ng, nb, LRU_BW, LRU_BW)
        eye = jnp.eye(nb, dtype=w.dtype)
        return jnp.einsum("gakj,ab->gakbj", w, eye).reshape(ng, cwid, cwid)

    wg = jnp.concatenate([block_diag(rw[0]), block_diag(iw[0]), block_diag(rw[1]), block_diag(iw[1])], axis=-1)
    bg = 0.5 * jnp.concatenate([rb_[0].reshape(ng, cwid), ib[0].reshape(ng, cwid),
                                rb_[1].reshape(ng, cwid), ib[1].reshape(ng, cwid)], axis=-1)[:, None, :]
    spl = (0.5 * RGLRU_C) * jax.nn.softplus(-lam.astype(F32)).reshape(2, ng, cwid).transpose(1, 0, 2)
    return dict(conv_w=conv_w, conv_b=conv_b, wg=(0.5 * wg).astype(BF16), bg=bg, spl=spl)


def _swiglu_acc(h, w1_ref, w3_ref, w2_ref, nf):
    f = w1_ref.shape[-1]
    step = -(-f // (nf * MXU_DEPTH)) * MXU_DEPTH
    edges = list(range(0, f, step)) + [f]
    acc = None
    for lo, hi in zip(edges[:-1], edges[1:]):
        a = jnp.dot(h, w1_ref[:, lo:hi], preferred_element_type=F32)
        b = jnp.dot(h, w3_ref[:, lo:hi], preferred_element_type=F32)
        pj = (a * jax.nn.sigmoid(a) * b).astype(BF16)
        o = jnp.dot(pj, w2_ref[lo:hi, :], preferred_element_type=F32)
        acc = o if acc is None else acc + o
    return acc


def _outproj_kernel(route, final, y_ref, g_ref, x_ref, mod_ref, lnw_ref, wo_ref, n2w_ref, *rest):
    if route:
        rw_ref, x1_ref, h2_ref, rt_ref = rest
    elif final:
        w1_ref, w3_ref, w2_ref, fw_ref, o_ref = rest
    else:
        w1_ref, w3_ref, w2_ref, o_ref = rest
    m = mod_ref[0]
    gl = (_rms(g_ref[0].astype(F32)) * lnw_ref[...]).astype(BF16)
    dy = y_ref.shape[-1]
    proj = (jnp.dot(y_ref[0], wo_ref[0:dy], preferred_element_type=F32)
            + jnp.dot(gl, wo_ref[dy:], preferred_element_type=F32))
    x1 = x_ref[0] + m[2:3] * proj
    h2 = _rms(x1) * n2w_ref[...] * (1.0 + m[4:5]) + m[3:4]
    if not route:
        out = x1 + m[5:6] * _swiglu_acc(h2.astype(BF16), w1_ref, w3_ref, w2_ref, 2)
        if final:
            out = _rms(out) * fw_ref[...]
        o_ref[0] = out
        return
    x1_ref[0] = x1
    h2_ref[0] = _pack_pairs(h2)
    h_hi = h2.astype(BF16)
    h_lo = (h2 - h_hi.astype(F32)).astype(BF16)
    r_hi = jnp.dot(h_hi, rw_ref[...], preferred_element_type=F32)
    logits = (r_hi[:, :LANE] + r_hi[:, LANE:]
              + jnp.dot(h_lo, rw_ref[:, :LANE], preferred_element_type=F32))
    lane = lax.broadcasted_iota(jnp.int32, logits.shape, 1).astype(F32)
    neg = -jnp.inf
    logits = jnp.where(lane < N_EXPERTS, logits, neg)
    m1 = jnp.max(logits, axis=-1, keepdims=True)
    i1 = jnp.min(jnp.where(logits == m1, lane, float(LANE)), axis=-1, keepdims=True)
    rest_l = jnp.where(lane == i1, neg, logits)
    m2 = jnp.max(rest_l, axis=-1, keepdims=True)
    i2 = jnp.min(jnp.where(rest_l == m2, lane, float(LANE)), axis=-1, keepdims=True)
    e = jnp.exp(m2 - m1)
    p1 = 1.0 / (1.0 + e)
    p2 = e / (1.0 + e)
    rt_ref[0] = jnp.where(lane == 0.0, i1, jnp.where(lane == 1.0, i2,
                          jnp.where(lane == 2.0, p1, jnp.where(lane == 3.0, p2, 0.0))))


def _outproj(y, g, x, mod_l, mod_row, lru_norm_w, w_out, norm2_w, tm, router_w=None, ffn_w=None, final_w=None):
    bsz, s, d = x.shape
    route = router_w is not None
    final = final_w is not None and not route
    in_specs = [
        pl.BlockSpec((1, tm, y.shape[-1]), lambda b, i: (b, i, 0)),
        pl.BlockSpec((1, tm, g.shape[-1]), lambda b, i: (b, i, 0)),
        pl.BlockSpec((1, tm, d), lambda b, i: (b, i, 0)),
        pl.BlockSpec((1, N_MOD, d), lambda b, i: (mod_row(b), 0, 0)),
        _resident((1, g.shape[-1])),
        _resident(w_out.shape),
        _resident((1, d)),
    ]
    args = [y, g, x, mod_l, lru_norm_w[None], w_out, norm2_w[None]]
    tok = pl.BlockSpec((1, tm, d), lambda b, i: (b, i, 0))
    tok_shape = jax.ShapeDtypeStruct((bsz, s, d), F32)
    if route:
        rw = jnp.zeros((d, LANE), F32).at[:, :N_EXPERTS].set(router_w)
        rw_hi = rw.astype(BF16)
        in_specs.append(_resident((d, 2 * LANE)))
        args.append(jnp.concatenate([rw_hi, (rw - rw_hi.astype(F32)).astype(BF16)], axis=1))
        out_specs = [tok, pl.BlockSpec((1, tm, d // 2), lambda b, i: (b, i, 0)),
                     pl.BlockSpec((1, tm, LANE), lambda b, i: (b, i, 0))]
        out_shape = [tok_shape, jax.ShapeDtypeStruct((bsz, s, d // 2), jnp.uint32),
                     jax.ShapeDtypeStruct((bsz, s, LANE), F32)]
    else:
        in_specs += [_resident(w.shape) for w in ffn_w]
        args += list(ffn_w)
        if final:
            in_specs.append(_resident((1, d)))
            args.append(final_w[None])
        out_specs, out_shape = tok, tok_shape
    return pl.pallas_call(
        functools.partial(_outproj_kernel, route, final),
        grid=(bsz, s // tm),
        in_specs=in_specs,
        out_specs=out_specs,
        out_shape=out_shape,
        compiler_params=_cparams("parallel", "arbitrary"),
        name="outproj_route" if route else "outproj_ffn",
    )(*args)


def _sc_worker_id():
    return lax.axis_index("subcore") * SC_CORES + lax.axis_index("core")


def _sc_mesh():
    return plsc.VectorSubcoreMesh(core_axis_name="core", subcore_axis_name="subcore")


def _sc_scatter2(x, i0, i1, n_rows):
    m, d = x.shape
    per_w = m // SC_WORKERS
    assert per_w % SC_WINDOW == 0

    @functools.partial(
        pl.kernel, out_type=jax.ShapeDtypeStruct((n_rows, d), x.dtype), mesh=_sc_mesh(),
        scratch_types=[pltpu.VMEM((SC_WINDOW,), jnp.int32), pltpu.VMEM((SC_WINDOW, d), x.dtype),
                       pltpu.SemaphoreType.DMA])
    def scatter(x_hbm, i0_hbm, i1_hbm, o_hbm, idx_v, rows_v, sem):
        wid = _sc_worker_id()

        @pl.loop(0, per_w // SC_WINDOW)
        def _(j):
            base = wid * per_w + j * SC_WINDOW
            pltpu.sync_copy(x_hbm.at[pl.ds(base, SC_WINDOW)], rows_v)
            for i_hbm in (i0_hbm, i1_hbm):
                pltpu.sync_copy(i_hbm.at[pl.ds(base, SC_WINDOW)], idx_v)
                pltpu.async_copy(rows_v, o_hbm.at[idx_v], sem).wait()

    return scatter(x, i0, i1)


def _sc_gather2(table, i0, i1):
    m = i0.shape[0]
    d = table.shape[1]
    per_w = m // SC_WORKERS
    assert per_w % SC_WINDOW == 0
    out = jax.ShapeDtypeStruct((m, d), table.dtype)

    @functools.partial(
        pl.kernel, out_type=(out, out), mesh=_sc_mesh(),
        scratch_types=[pltpu.VMEM((SC_WINDOW,), jnp.int32), pltpu.VMEM((SC_WINDOW, d), table.dtype),
                       pltpu.SemaphoreType.DMA])
    def gather(t_hbm, i0_hbm, i1_hbm, a_hbm, b_hbm, idx_v, rows_v, sem):
        wid = _sc_worker_id()

        @pl.loop(0, per_w // SC_WINDOW)
        def _(j):
            base = wid * per_w + j * SC_WINDOW
            for i_hbm, o_hbm in ((i0_hbm, a_hbm), (i1_hbm, b_hbm)):
                pltpu.sync_copy(i_hbm.at[pl.ds(base, SC_WINDOW)], idx_v)
                pltpu.async_copy(t_hbm.at[idx_v], rows_v, sem).wait()
                pltpu.sync_copy(rows_v, o_hbm.at[pl.ds(base, SC_WINDOW)])

    return gather(table, i0, i1)


def _experts_kernel(nf, te_ref, tv_ref, h_ref, w1_ref, w3_ref, w2_ref, o_ref):
    valid = tv_ref[pl.program_id(0)]

    @pl.when(valid > 0)
    def _():
        row = lax.broadcasted_iota(jnp.int32, h_ref.shape, 0)
        words = jnp.where(row < valid, h_ref[...], jnp.uint32(0))
        lo, hi = _unpack_pairs(words)
        h = jnp.concatenate([lo, hi], axis=1).astype(BF16)
        o_ref[...] = _pack_pairs(_swiglu_acc(h, w1_ref.at[0], w3_ref.at[0], w2_ref.at[0], nf))

    @pl.when(valid <= 0)
    def _():
        o_ref[...] = jnp.zeros_like(o_ref)


def _experts(h_sorted, tile_expert, tile_valid, w1, w3, w2, tile):
    n, dw = h_sorted.shape
    d, f = w1.shape[1:]
    grid_spec = pltpu.PrefetchScalarGridSpec(
        num_scalar_prefetch=2,
        grid=(n // tile,),
        in_specs=[
            pl.BlockSpec((tile, dw), lambda i, te, tv: (i, 0)),
            pl.BlockSpec((1, d, f), lambda i, te, tv: (te[i], 0, 0), pipeline_mode=pl.Buffered(1)),
            pl.BlockSpec((1, d, f), lambda i, te, tv: (te[i], 0, 0), pipeline_mode=pl.Buffered(1)),
            pl.BlockSpec((1, f, d), lambda i, te, tv: (te[i], 0, 0), pipeline_mode=pl.Buffered(1)),
        ],
        out_specs=pl.BlockSpec((tile, dw), lambda i, te, tv: (i, 0)),
    )
    return pl.pallas_call(
        functools.partial(_experts_kernel, 2),
        grid_spec=grid_spec,
        out_shape=jax.ShapeDtypeStruct((n, dw), jnp.uint32),
        compiler_params=_cparams("arbitrary"),
        name="experts",
    )(tile_expert, tile_valid, h_sorted, w1, w3, w2)


def _combine_kernel(x1_ref, ya_ref, yb_ref, rt_ref, mod_ref, fw_ref, o_ref):
    m = mod_ref[0]
    rt = rt_ref[0]
    a_lo, a_hi = _unpack_pairs(ya_ref[0])
    b_lo, b_hi = _unpack_pairs(yb_ref[0])
    p1, p2 = rt[:, 2:3], rt[:, 3:4]
    moe = jnp.concatenate([p1 * a_lo + p2 * b_lo, p1 * a_hi + p2 * b_hi], axis=1)
    o_ref[0] = _rms(x1_ref[0] + m[5:6] * moe) * fw_ref[...]


def _moe(x1, h2, route, mod_l, mod_row, w1, w3, w2, final_w, tm, tile):
    bsz, s, d = x1.shape
    n = bsz * s
    ex = route[..., :2].astype(jnp.int32).reshape(2 * n)
    onehot = (ex[:, None] == jnp.arange(N_EXPERTS, dtype=jnp.int32)[None, :]).astype(jnp.int32)
    csum = jnp.cumsum(onehot, axis=0)
    cnt = csum[-1]
    rank = jnp.sum((csum - onehot) * onehot, axis=1)
    padded = ((cnt + tile - 1) // tile) * tile
    ends = jnp.cumsum(padded)
    pos = (ends - padded)[ex] + rank
    n_rows = 2 * n + N_EXPERTS * tile
    tile_start = jnp.arange(n_rows // tile, dtype=jnp.int32) * tile
    tile_expert = jnp.minimum(jnp.searchsorted(ends, tile_start, side="right"), N_EXPERTS - 1).astype(jnp.int32)
    group_end = (ends - padded + cnt)[tile_expert]
    tile_valid = jnp.clip(group_end - tile_start, 0, tile).astype(jnp.int32)

    pos2 = pos.reshape(n, 2)
    dw = h2.shape[-1]
    h_sorted = _sc_scatter2(h2.reshape(n, dw), pos2[:, 0], pos2[:, 1], n_rows)
    y_sorted = _experts(h_sorted, tile_expert, tile_valid, w1, w3, w2, tile)
    ya, yb = _sc_gather2(y_sorted, pos2[:, 0], pos2[:, 1])
    ya = ya.reshape(bsz, s, dw)
    yb = yb.reshape(bsz, s, dw)
    tok = pl.BlockSpec((1, tm, d), lambda b, i: (b, i, 0))
    words = pl.BlockSpec((1, tm, dw), lambda b, i: (b, i, 0))
    return pl.pallas_call(
        _combine_kernel,
        grid=(bsz, s // tm),
        in_specs=[tok, words, words,
                  pl.BlockSpec((1, tm, LANE), lambda b, i: (b, i, 0)),
                  pl.BlockSpec((1, N_MOD, d), lambda b, i: (mod_row(b), 0, 0)),
                  _resident((1, d))],
        out_specs=tok,
        out_shape=jax.ShapeDtypeStruct((bsz, s, d), F32),
        compiler_params=_cparams("parallel", "arbitrary"),
        name="moe_combine",
    )(x1, ya, yb, route, mod_l, final_w[None])


def _pick_tile(s, pref):
    t = min(s, pref)
    while s % t:
        t //= 2
    return t


def kernel(x, c, ctx, c_ctx, mod_w, mod_b, norm1_w, norm2_w, w_in, ssd_conv_w, ssd_conv_b, ssd_dt_bias,
           ssd_a_log, ssd_d, ssd_norm_w, lru_conv_w, lru_conv_b, lru_rw, lru_rb, lru_iw, lru_ib, lru_lambda,
           lru_norm_w, w_out, ffn_w1, ffn_w3, ffn_w2, router_w, moe_w1, moe_w3, moe_w2, final_norm_w):
    depth = w_in.shape[0]
    bsz, seq, d = x.shape
    lc = ctx.shape[1]
    rows = seq // GRID_W
    heads = ssd_dt_bias.shape[-1]
    d_ssd = heads * SSD_HEADDIM
    d_xbc = ssd_conv_w.shape[-1]
    d_lru = lru_conv_w.shape[-1]
    splits = (d_ssd, d_xbc, d_lru, d_lru)
    off_dt = d_ssd + d_xbc
    off_lx = off_dt + 2 * heads
    ctx_cols = 8
    ctx_rows = lc // ctx_cols
    lru_cw = 256

    mod = _modulation(c, c_ctx, mod_w, mod_b)
    lat_row = lambda b: b
    ctx_row = lambda b: bsz
    tm_lat = _pick_tile(seq, 512)
    tm_ctx = _pick_tile(lc, 512)
    tb_lat = _pick_tile(seq, 1024)
    tb_ctx = _pick_tile(lc, 512)
    gw = d_ssd // SSD_GROUPS

    xc = ctx
    for l in range(depth):
        last = l == depth - 1
        mod_l = mod[l]
        w_l = w_in[l].astype(BF16)
        w_perm = jnp.concatenate([w_l[:, :off_dt], w_l[:, off_lx:], w_l[:, off_dt:off_lx],
                                  jnp.zeros((d, LANE - 2 * heads), BF16)], axis=1)
        ssd_p = dict(dt_bias=ssd_dt_bias[l], a_log=ssd_a_log[l], d_skip=ssd_d[l], norm_w=ssd_norm_w[l])
        lru_p = _lru_params(lru_conv_w[l], lru_conv_b[l], lru_rw[l], lru_rb[l], lru_iw[l], lru_ib[l],
                            lru_lambda[l], lru_cw)
        wo = w_out[l].astype(BF16)

        zc, xbcc, lxc, lgc, dtc = _inproj(xc, mod_l, ctx_row, norm1_w[l], w_perm, ssd_conv_w[l], ssd_conv_b[l],
                                          splits, 2 * heads, tm_ctx)
        zl, xbcl, lxl, lgl, dtl = _inproj(x, mod_l, lat_row, norm1_w[l], w_perm, ssd_conv_w[l], ssd_conv_b[l],
                                          splits, 2 * heads, tm_lat)

        zero_s = jnp.zeros((bsz, SSD_GROUPS, SSD_STATE, gw), F32)
        yc_ssd, s_f, s_b = _ssd_bidir(zc, xbcc, dtc, ssd_p, zero_s, zero_s, tb_ctx)
        yl_ssd, _, _ = _ssd_bidir(zl, xbcl, dtl, ssd_p, s_f, s_b, tb_lat)

        to_grid = lambda v: v.reshape(bsz, ctx_cols, ctx_rows, d_lru).transpose(0, 2, 1, 3)
        gc, h_fin = _lru(to_grid(lxc), to_grid(lgc), lru_p, jnp.zeros((bsz, 2, d_lru), F32), ctx_rows, True)
        gl, _ = _lru(lxl.reshape(bsz, rows, GRID_W, d_lru), lgl.reshape(bsz, rows, GRID_W, d_lru), lru_p,
                     h_fin, 8, False)
        gl = gl.reshape(bsz, seq, d_lru)

        if l % 2 == 0:
            w1, w3, w2 = (ffn_w1[l // 2].astype(BF16), ffn_w3[l // 2].astype(BF16), ffn_w2[l // 2].astype(BF16))
            x_next = _outproj(yl_ssd, gl, x, mod_l, lat_row, lru_norm_w[l], wo, norm2_w[l], tm_lat,
                              ffn_w=(w1, w3, w2), final_w=final_norm_w if last else None)
            if not last:
                gc = gc.transpose(0, 2, 1, 3).reshape(bsz, lc, d_lru)
                xc = _outproj(yc_ssd, gc, xc, mod_l, ctx_row, lru_norm_w[l], wo, norm2_w[l], tm_ctx,
                              ffn_w=(w1, w3, w2))
            x = x_next
        else:
            w1, w3, w2 = (moe_w1[l // 2].astype(BF16), moe_w3[l // 2].astype(BF16), moe_w2[l // 2].astype(BF16))
            assert last, "a routed layer that is not the last layer is not implemented"
            x1, h2, route = _outproj(yl_ssd, gl, x, mod_l, lat_row, lru_norm_w[l], wo, norm2_w[l], tm_lat,
                                     router_w=router_w[l // 2])
            x = _moe(x1, h2, route, mod_l, lat_row, w1, w3, w2, final_norm_w, tm_lat, 512)
    return x
```
